```python
import math
import jax, jax.numpy as jnp
from jax import lax
import numpy as np

D_MODEL = 1024
BATCH = 16
SEQ = 2048
DEPTH = 4

CHUNK = 64
QBLOCK = 128
HEAD_DIM = 64
N_GROUPS = 4
GROUP_HEADS = D_MODEL // (N_GROUPS * HEAD_DIM)
GROUP_WIDTH = GROUP_HEADS * HEAD_DIM
D_MIX = N_GROUPS * GROUP_WIDTH
IDX_HEADS = 8
IDX_DIM = 32
TOPK_MAX = 256
T5_BUCKETS = 32
T5_MAX_DIST = 128
Q_LORA = 192
KV_LORA = 128
MLA_NOPE = HEAD_DIM
MLA_ROPE = HEAD_DIM // 2
MLA_V = HEAD_DIM
ROPE_THETA = 10000.0
LEFT_CHUNKS = 8
MAX_REL = 256
N_EXPERTS = 64
TOP_K = 8
D_EXPERT = D_MODEL // 4
D_SHARED = D_EXPERT
ROUTED_SCALE = 2.5
ALPHA = (2 * DEPTH) ** 0.25
BETA = (8 * DEPTH) ** -0.25
LN_EPS = 1e-5
RMS_EPS = 1e-6
IN_SIZES = (3 * GROUP_WIDTH, GROUP_HEADS, 3 * GROUP_WIDTH, IDX_HEADS * IDX_DIM, IDX_DIM, IDX_HEADS,
            Q_LORA, KV_LORA, MLA_ROPE, 3 * GROUP_WIDTH)
IN_SPLITS = tuple(int(v) for v in np.cumsum(IN_SIZES)[:-1])
D_IN = sum(IN_SIZES)

kernel_name = 'hybrid_fox_dsa_mla_chunkband_moe_deepnorm'


def _layer_norm(x, g, b):
    xf = x.astype(jnp.float32)
    mu = jnp.mean(xf, axis=-1, keepdims=True)
    var = jnp.mean(jnp.square(xf - mu), axis=-1, keepdims=True)
    y = (xf - mu) * lax.rsqrt(var + LN_EPS)
    return (y * g.astype(jnp.float32) + b.astype(jnp.float32)).astype(x.dtype)


def _rms_norm(x, g):
    xf = x.astype(jnp.float32)
    y = xf * lax.rsqrt(jnp.mean(jnp.square(xf), axis=-1, keepdims=True) + RMS_EPS)
    return (y * g.astype(jnp.float32)).astype(x.dtype)


def _rope(x, pos):
    half = x.shape[-1] // 2
    inv_freq = ROPE_THETA ** (-jnp.arange(half, dtype=jnp.float32) / half)
    ang = pos.astype(jnp.float32)[:, None] * inv_freq[None, :]
    shape = (1, x.shape[1]) + (1,) * (x.ndim - 3) + (half,)
    cos = jnp.cos(ang).reshape(shape)
    sin = jnp.sin(ang).reshape(shape)
    x1 = x[..., :half].astype(jnp.float32)
    x2 = x[..., half:].astype(jnp.float32)
    return jnp.concatenate([x1 * cos - x2 * sin, x2 * cos + x1 * sin], axis=-1).astype(x.dtype)


def _t5_bucket(rel):
    half = T5_BUCKETS // 2
    max_exact = half // 2
    n = jnp.abs(rel)
    large = max_exact + (jnp.log(jnp.maximum(n, 1).astype(jnp.float32) / max_exact)
                         / math.log(T5_MAX_DIST / max_exact) * (half - max_exact)).astype(jnp.int32)
    large = jnp.minimum(large, half - 1)
    return jnp.where(rel > 0, half, 0) + jnp.where(n < max_exact, n, large)


def _query_block_attention(q, k, v, scale, block_bias):
    b, s, h, dk = q.shape
    dv = v.shape[-1]
    nb = s // QBLOCK
    q_blocks = jnp.moveaxis(q.reshape(b, nb, QBLOCK, h, dk), 1, 0)

    def one_block(args):
        i, q_i = args
        logits = jnp.einsum('bqhd,bshd->bhqs', q_i, k).astype(jnp.float32) * scale
        p = jax.nn.softmax(logits + block_bias(i), axis=-1).astype(v.dtype)
        return jnp.einsum('bhqs,bshd->bqhd', p, v)

    out = lax.map(one_block, (jnp.arange(nb), q_blocks))
    return jnp.moveaxis(out, 0, 1).reshape(b, s, h * dv)


def _fox_group(qkv, f_logit, b_f):
    b, s, _ = qkv.shape
    q, k, v = [t.reshape(b, s, GROUP_HEADS, HEAD_DIM) for t in jnp.split(qkv, 3, axis=-1)]
    log_f = jax.nn.log_sigmoid(f_logit.astype(jnp.float32) + b_f.astype(jnp.float32))
    cum = jnp.transpose(jnp.cumsum(log_f, axis=1), (0, 2, 1))
    s_pos = jnp.arange(s)

    def block_bias(i):
        t_pos = i * QBLOCK + jnp.arange(QBLOCK)
        cum_q = lax.dynamic_slice_in_dim(cum, i * QBLOCK, QBLOCK, axis=2)
        decay = cum_q[..., :, None] - cum[..., None, :]
        return jnp.where(s_pos[None, :] <= t_pos[:, None], decay, -jnp.inf)

    return _query_block_attention(q, k, v, HEAD_DIM ** -0.5, block_bias)


def _dsa_group(qkv, idx_q, idx_k, idx_w, t5_table):
    b, s, _ = qkv.shape
    nc = s // CHUNK
    n_sel = min(TOPK_MAX, s // 4)
    q, k, v = jnp.split(qkv, 3, axis=-1)
    q = q.reshape(b, s, GROUP_HEADS, HEAD_DIM)
    iq = idx_q.reshape(b, s, IDX_HEADS, IDX_DIM)
    iw = idx_w.astype(jnp.float32) * (IDX_HEADS ** -0.5 * IDX_DIM ** -0.5)
    s_pos = jnp.arange(s)
    gather_rows = jax.vmap(lambda rows, ids: rows[ids])

    def to_chunks(a):
        return jnp.moveaxis(a.reshape((b, nc, CHUNK) + a.shape[2:]), 1, 0)

    def one_chunk(args):
        c, q_c, iq_c, iw_c = args
        limit = (c + 1) * CHUNK
        dots = jnp.einsum('bqhd,bsd->bqhs', iq_c, idx_k).astype(jnp.float32)
        score = jnp.einsum('bqhs,bqh->bqs', jax.nn.relu(dots), iw_c)
        score = jnp.where(s_pos < limit, score, -jnp.inf)
        _, sel = lax.top_k(score, n_sel)
        k_sel = gather_rows(k, sel).reshape(b, CHUNK, n_sel, GROUP_HEADS, HEAD_DIM)
        v_sel = gather_rows(v, sel).reshape(b, CHUNK, n_sel, GROUP_HEADS, HEAD_DIM)
        logits = jnp.einsum('bqhd,bqkhd->bhqk', q_c, k_sel).astype(jnp.float32) * HEAD_DIM ** -0.5
        rel = sel - (c * CHUNK + jnp.arange(CHUNK))[None, :, None]
        pos_bias = jnp.moveaxis(t5_table[_t5_bucket(rel)], -1, 1).astype(jnp.float32)
        logits = jnp.where((sel < limit)[:, None], logits + pos_bias, -jnp.inf)
        p = jax.nn.softmax(logits, axis=-1).astype(v.dtype)
        return jnp.einsum('bhqk,bqkhd->bqhd', p, v_sel)

    out = lax.map(one_chunk, (jnp.arange(nc), to_chunks(q), to_chunks(iq), to_chunks(iw)))
    return jnp.moveaxis(out, 0, 1).reshape(b, s, GROUP_WIDTH)


def _mla_group(c_q, c_kv, k_r, q_norm, kv_norm, w_uq, w_ukv, pos):
    b, s, _ = c_q.shape
    q = (_rms_norm(c_q, q_norm) @ w_uq).reshape(b, s, GROUP_HEADS, MLA_NOPE + MLA_ROPE)
    q = jnp.concatenate([q[..., :MLA_NOPE], _rope(q[..., MLA_NOPE:], pos)], axis=-1)
    kv = (_rms_norm(c_kv, kv_norm) @ w_ukv).reshape(b, s, GROUP_HEADS, MLA_NOPE + MLA_V)
    k_rope = jnp.broadcast_to(_rope(k_r, pos)[:, :, None, :], (b, s, GROUP_HEADS, MLA_ROPE))
    k = jnp.concatenate([kv[..., :MLA_NOPE], k_rope], axis=-1)
    v = kv[..., MLA_NOPE:]
    key_chunk = jnp.arange(s) // CHUNK

    def block_bias(i):
        q_chunk = (i * QBLOCK + jnp.arange(QBLOCK)) // CHUNK
        return jnp.where(key_chunk[None, :] <= q_chunk[:, None], 0.0, -jnp.inf)

    return _query_block_attention(q, k, v, (MLA_NOPE + MLA_ROPE) ** -0.5, block_bias)


def _chunk_band_group(qkv, rel_table):
    b, s, _ = qkv.shape
    nc = s // CHUNK
    band = (LEFT_CHUNKS + 1) * CHUNK
    q, k, v = [t.reshape(b, nc, CHUNK, GROUP_HEADS, HEAD_DIM) for t in jnp.split(qkv, 3, axis=-1)]

    def key_band(a):
        a_pad = jnp.pad(a, ((0, 0), (LEFT_CHUNKS, 0), (0, 0), (0, 0), (0, 0)))
        return jnp.concatenate([a_pad[:, j:j + nc] for j in range(LEFT_CHUNKS + 1)], axis=2)

    k_band, v_band = key_band(k), key_band(v)
    logits = jnp.einsum('bcqhd,bckhd->bchqk', q, k_band).astype(jnp.float32) * HEAD_DIM ** -0.5
    r = jnp.arange(band)
    dist = jnp.arange(CHUNK)[:, None] + LEFT_CHUNKS * CHUNK - r[None, :]
    rel_bias = rel_table[:, jnp.clip(dist, -MAX_REL, MAX_REL) + MAX_REL].astype(jnp.float32)
    key_pos = jnp.arange(nc)[:, None] * CHUNK - LEFT_CHUNKS * CHUNK + r[None, :]
    valid = (key_pos >= 0)[None, :, None, None, :]
    logits = jnp.where(valid, logits + rel_bias[None, None], -jnp.inf)
    p = jax.nn.softmax(logits, axis=-1).astype(v.dtype)
    return jnp.einsum('bchqk,bckhd->bcqhd', p, v_band).reshape(b, s, GROUP_WIDTH)


def _moe(x, w_router, router_bias, w_gu, w_down, w_sh_gu, w_sh_down):
    b, s, d = x.shape
    xt = x.reshape(b * s, d)
    n = xt.shape[0]
    scores = jax.nn.sigmoid((xt @ w_router).astype(jnp.float32))
    _, sel = lax.top_k(scores + router_bias.astype(jnp.float32), TOP_K)
    s_sel = jnp.take_along_axis(scores, sel, axis=-1)
    g = s_sel / jnp.sum(s_sel, axis=-1, keepdims=True) * ROUTED_SCALE
    gates_t = jnp.zeros_like(scores).at[jnp.arange(n)[:, None], sel].set(g).T.astype(x.dtype)

    def add_expert(e, acc):
        h = xt @ w_gu[e]
        a = jax.nn.silu(h[:, :D_EXPERT]) * h[:, D_EXPERT:]
        return acc + (a * gates_t[e][:, None]) @ w_down[e]

    routed = lax.fori_loop(0, N_EXPERTS, add_expert, jnp.zeros_like(xt))
    hs = xt @ w_sh_gu
    shared = (jax.nn.silu(hs[:, :D_SHARED]) * hs[:, D_SHARED:]) @ w_sh_down
    return (routed + shared).reshape(b, s, d)


def setup_inputs(seed: int = 0) -> dict:
    key = jax.random.key(seed)
    ks = jax.random.split(key, 20)

    def nrm(k, shape, scale):
        return jax.random.normal(k, shape, jnp.float32) * scale

    return {
        'x': nrm(ks[0], (BATCH, SEQ, D_MODEL), 1.0),
        'w_in': nrm(ks[1], (DEPTH, D_MODEL, D_IN), D_MODEL ** -0.5),
        'b_forget': jax.random.uniform(ks[2], (DEPTH, GROUP_HEADS), jnp.float32, 1.0, 4.0),
        'mla_q_norm': 1.0 + nrm(ks[3], (DEPTH, Q_LORA), 0.02),
        'mla_kv_norm': 1.0 + nrm(ks[4], (DEPTH, KV_LORA), 0.02),
        'w_mla_uq': nrm(ks[5], (DEPTH, Q_LORA, GROUP_HEADS * (MLA_NOPE + MLA_ROPE)), Q_LORA ** -0.5),
        'w_mla_ukv': nrm(ks[6], (DEPTH, KV_LORA, GROUP_HEADS * (MLA_NOPE + MLA_V)), KV_LORA ** -0.5),
        't5_rel_bias': nrm(ks[7], (T5_BUCKETS, GROUP_HEADS), 0.5),
        'chunk_rel_bias': nrm(ks[8], (DEPTH, GROUP_HEADS, 2 * MAX_REL + 1), 0.5),
        'w_out': nrm(ks[9], (DEPTH, D_MIX, D_MODEL), BETA * D_MIX ** -0.5),
        'ln1_g': 1.0 + nrm(ks[10], (DEPTH, D_MODEL), 0.02),
        'ln1_b': nrm(ks[11], (DEPTH, D_MODEL), 0.02),
        'w_router': nrm(ks[12], (DEPTH, D_MODEL, N_EXPERTS), D_MODEL ** -0.5),
        'router_bias': nrm(ks[13], (DEPTH, N_EXPERTS), 0.01),
        'w_exp_gu': nrm(ks[14], (DEPTH, N_EXPERTS, D_MODEL, 2 * D_EXPERT), D_MODEL ** -0.5),
        'w_exp_down': nrm(ks[15], (DEPTH, N_EXPERTS, D_EXPERT, D_MODEL), BETA * D_EXPERT ** -0.5),
        'w_sh_gu': nrm(ks[16], (DEPTH, D_MODEL, 2 * D_SHARED), D_MODEL ** -0.5),
        'w_sh_down': nrm(ks[17], (DEPTH, D_SHARED, D_MODEL), BETA * D_SHARED ** -0.5),
        'ln2_g': 1.0 + nrm(ks[18], (DEPTH, D_MODEL), 0.02),
        'ln2_b': nrm(ks[19], (DEPTH, D_MODEL), 0.02),
    }


def reference(x, w_in, b_forget, mla_q_norm, mla_kv_norm, w_mla_uq, w_mla_ukv, t5_rel_bias,
              chunk_rel_bias, w_out, ln1_g, ln1_b, w_router, router_bias, w_exp_gu, w_exp_down,
              w_sh_gu, w_sh_down, ln2_g, ln2_b):
    pos = jnp.arange(x.shape[1])
    for layer in range(DEPTH):
        (fox_qkv, fox_f, dsa_qkv, idx_q, idx_k, idx_w,
         c_q, c_kv, k_r, band_qkv) = jnp.split(x @ w_in[layer], IN_SPLITS, axis=-1)
        mixed = jnp.concatenate([
            _fox_group(fox_qkv, fox_f, b_forget[layer]),
            _dsa_group(dsa_qkv, idx_q, idx_k, idx_w, t5_rel_bias),
            _mla_group(c_q, c_kv, k_r, mla_q_norm[layer], mla_kv_norm[layer],
                       w_mla_uq[layer], w_mla_ukv[layer], pos),
            _chunk_band_group(band_qkv, chunk_rel_bias[layer]),
        ], axis=-1)
        x = _layer_norm(ALPHA * x + mixed @ w_out[layer], ln1_g[layer], ln1_b[layer])
        ffn = _moe(x, w_router[layer], router_bias[layer], w_exp_gu[layer], w_exp_down[layer],
                   w_sh_gu[layer], w_sh_down[layer])
        x = _layer_norm(ALPHA * x + ffn, ln2_g[layer], ln2_b[layer])
    return x
```

```python
import functools
import math

import numpy as np
import jax
import jax.numpy as jnp
from jax import lax
from jax.experimental import pallas as pl
from jax.experimental.pallas import tpu as pltpu

D_MODEL = 1024
CHUNK = 64
HEAD_DIM = 64
GROUP_HEADS = 4
GROUP_WIDTH = GROUP_HEADS * HEAD_DIM
IDX_HEADS = 8
IDX_DIM = 32
TOPK_MAX = 256
T5_BUCKETS = 32
T5_MAX_DIST = 128
Q_LORA = 192
KV_LORA = 128
MLA_NOPE = 64
MLA_ROPE = 32
ROPE_THETA = 10000.0
LEFT_CHUNKS = 8
MAX_REL = 256
N_EXPERTS = 64
TOP_K = 8
D_EXPERT = 256
D_SHARED = 256
ROUTED_SCALE = 2.5
LN_EPS = 1e-5
RMS_EPS = 1e-6

LANES = 128
NEG = -1e30
BISECT_BOUND = 512
VMEM_LIMIT = 56 * 1024 * 1024

MISC_W = 512
CQ_OFF, CKV_OFF, SMALL_OFF = 0, 256, 384
FF_OFF, IW_OFF, IK_OFF, KR_OFF = 384, 392, 416, 448

BF16 = jnp.bfloat16
F32 = jnp.float32


def _dot(a, b):
    return jnp.dot(a, b, preferred_element_type=F32)


def _dot_nt(a, b):
    return lax.dot_general(a, b, (((1,), (1,)), ((), ())), preferred_element_type=F32)


def _dot_f32(a, b):
    return jnp.dot(a, b, preferred_element_type=F32, precision=lax.Precision.HIGHEST)


def _params(*sem):
    return pltpu.CompilerParams(dimension_semantics=sem, vmem_limit_bytes=VMEM_LIMIT)


def _layer_norm(y, g, b):
    mu = jnp.mean(y, axis=-1, keepdims=True)
    yc = y - mu
    var = jnp.mean(yc * yc, axis=-1, keepdims=True)
    return yc * lax.rsqrt(var + LN_EPS) * g + b


def _chunk_end(pos):
    return (jnp.right_shift(pos, CHUNK.bit_length() - 1) + 1) * CHUNK


def _silu(x):
    return x / (1.0 + jnp.exp(-x))


def _softmax_pv(s, v):
    m = jnp.max(s, axis=1, keepdims=True)
    p = jnp.exp(s - m)
    l = jnp.sum(p, axis=1, keepdims=True)
    return _dot(p.astype(BF16), v) / l


def _inproj_kernel(x_ref, w_ref, fox_ref, dsa_ref, band_ref, iq_ref, misc_ref):
    xb = x_ref[...].astype(BF16)
    off = 0
    for ref in (fox_ref, dsa_ref, band_ref, iq_ref, misc_ref):
        w = ref.shape[1]
        ref[...] = _dot(xb, w_ref[:, off:off + w]).astype(ref.dtype)
        off += w


def _inproj(x2, w_re, tm=512):
    n = x2.shape[0]
    widths = (3 * GROUP_WIDTH, 3 * GROUP_WIDTH, 3 * GROUP_WIDTH, IDX_HEADS * IDX_DIM, MISC_W)
    dtypes = (BF16, BF16, BF16, BF16, F32)
    return pl.pallas_call(
        _inproj_kernel,
        grid=(n // tm,),
        in_specs=[pl.BlockSpec((tm, D_MODEL), lambda i: (i, 0)),
                  pl.BlockSpec(w_re.shape, lambda i: (0, 0))],
        out_specs=[pl.BlockSpec((tm, w), lambda i: (i, 0)) for w in widths],
        out_shape=[jax.ShapeDtypeStruct((n, w), d) for w, d in zip(widths, dtypes)],
        compiler_params=_params("parallel"),
        name="inproj",
    )(x2, w_re)


def _cum_kernel(f_ref, b_ref, o_ref):
    x = f_ref[...] + b_ref[...]
    acc = jnp.minimum(x, 0.0) - jnp.log(1.0 + jnp.exp(-jnp.abs(x)))
    s = x.shape[1]
    lane = lax.broadcasted_iota(jnp.int32, x.shape, 1)
    d = 1
    while d < s:
        acc = acc + jnp.where(lane >= d, pltpu.roll(acc, d, 1), 0.0)
        d *= 2
    o_ref[...] = acc


def _fox_cum(f_t, b_col):
    b, r, s = f_t.shape
    return pl.pallas_call(
        _cum_kernel,
        grid=(b,),
        in_specs=[pl.BlockSpec((None, r, s), lambda i: (i, 0, 0)),
                  pl.BlockSpec((r, 1), lambda i: (0, 0))],
        out_specs=pl.BlockSpec((None, r, s), lambda i: (i, 0, 0)),
        out_shape=jax.ShapeDtypeStruct((b, r, s), F32),
        compiler_params=_params("parallel"),
        name="fox_cum",
    )(f_t, b_col)


def _fox_kernel(q_ref, k_ref, v_ref, cq_ref, ck_ref, o_ref, *, tq):
    i = pl.program_id(1)
    q, k, v = q_ref[...], k_ref[...], v_ref[...]
    s = k.shape[0]
    row = i * tq + lax.broadcasted_iota(jnp.int32, (tq, s), 0)
    col = lax.broadcasted_iota(jnp.int32, (tq, s), 1)
    causal = col <= row
    cq, ck = cq_ref[...], ck_ref[...]
    outs = []
    for h in range(GROUP_HEADS):
        sl = slice(h * HEAD_DIM, (h + 1) * HEAD_DIM)
        lg = _dot_nt(q[:, sl], k[:, sl]) * HEAD_DIM ** -0.5
        lg = lg + (cq[:, h:h + 1] - ck[h:h + 1, :])
        lg = jnp.where(causal, lg, NEG)
        outs.append(_softmax_pv(lg, v[:, sl]))
    o_ref[...] = jnp.concatenate(outs, axis=1).astype(o_ref.dtype)


def _fox_attn(qkv3, cum_col, cum_row, tq=256):
    b, s, _ = qkv3.shape
    gw = GROUP_WIDTH
    return pl.pallas_call(
        functools.partial(_fox_kernel, tq=tq),
        grid=(b, s // tq),
        in_specs=[pl.BlockSpec((None, tq, gw), lambda bi, i: (bi, i, 0)),
                  pl.BlockSpec((None, s, gw), lambda bi, i: (bi, 0, 1)),
                  pl.BlockSpec((None, s, gw), lambda bi, i: (bi, 0, 2)),
                  pl.BlockSpec((None, tq, 8), lambda bi, i: (bi, i, 0)),
                  pl.BlockSpec((None, 8, s), lambda bi, i: (bi, 0, 0))],
        out_specs=pl.BlockSpec((None, tq, gw), lambda bi, i: (bi, i, 0)),
        out_shape=jax.ShapeDtypeStruct((b, s, gw), BF16),
        compiler_params=_params("parallel", "parallel"),
        name="fox_attn",
    )(qkv3, qkv3, qkv3, cum_col, cum_row)


def _dsa_kernel(q_ref, k_ref, v_ref, iq_ref, ik_ref, iw_ref, t5_ref, o_ref, sc_ref, *, tq, n_sel):
    m = pl.program_id(1)
    s = k_ref.shape[0]
    row = lax.broadcasted_iota(jnp.int32, (tq, s), 0)
    col = lax.broadcasted_iota(jnp.int32, (tq, s), 1)
    limit = _chunk_end(m * tq + row)
    valid = col < limit

    iq, ik = iq_ref[...], ik_ref[...]
    iw = iw_ref[...] * (IDX_HEADS ** -0.5 * IDX_DIM ** -0.5)
    score = jnp.zeros((tq, s), F32)
    for h in range(IDX_HEADS):
        d = _dot(iq[:, h * IDX_DIM:(h + 1) * IDX_DIM], ik)
        score = score + jnp.maximum(d, 0.0) * iw[:, h:h + 1]
    score = jnp.where(valid, score, -jnp.inf)
    sc_ref[...] = score

    limit_col = _chunk_end(m * tq + lax.broadcasted_iota(jnp.int32, (tq, 1), 0))
    searching = limit_col > n_sel
    lo0 = jnp.where(searching, jnp.min(jnp.where(valid, score, jnp.inf), axis=1, keepdims=True), 0.0)
    hi0 = jnp.where(searching, jnp.max(score, axis=1, keepdims=True), 0.0)

    def unresolved(state):
        lo, hi, it = state
        return jnp.logical_and(jnp.max(jnp.where(lo < hi, 1.0, 0.0)) > 0.0, it < BISECT_BOUND)

    def bisect(state):
        lo, hi, it = state
        sc = sc_ref[...]
        mid = lo + (hi - lo) * 0.5
        mid = jnp.where(mid < hi, mid, lo)
        above = sc > mid
        cnt = jnp.sum(jnp.where(above, 1.0, 0.0), axis=1, keepdims=True)
        up = jnp.min(jnp.where(above, sc, jnp.inf), axis=1, keepdims=True)
        dn = jnp.max(jnp.where(above, -jnp.inf, sc), axis=1, keepdims=True)
        active = lo < hi
        more = cnt >= n_sel
        lo = jnp.where(active & more, up, lo)
        hi = jnp.where(active & jnp.logical_not(more), dn, hi)
        return lo, hi, it + 1

    thr, _, _ = lax.while_loop(unresolved, bisect, (lo0, hi0, jnp.int32(0)))
    thr = jnp.where(searching, thr, -jnp.inf)

    score = sc_ref[...]
    gt = score > thr
    eq = score == thr
    n_gt = jnp.sum(jnp.where(gt, 1.0, 0.0), axis=1, keepdims=True)
    n_eq = jnp.sum(jnp.where(eq, 1.0, 0.0), axis=1, keepdims=True)
    room = n_sel - n_gt
    tie_split = jnp.max(jnp.where(searching & (n_eq > room), 1.0, 0.0)) > 0.0

    def pick_with_ties(_):
        e = jnp.where(eq, 1.0, 0.0)
        inc = e
        d = 1
        while d < s:
            inc = inc + jnp.where(col >= d, pltpu.roll(inc, d, 1), 0.0)
            d *= 2
        keep_eq = jnp.where(inc - e < room, e, 0.0)
        return jnp.where(gt, 1.0, keep_eq)

    def pick_all(_):
        return jnp.where(gt, 1.0, jnp.where(eq, 1.0, 0.0))

    picked = lax.cond(tie_split, pick_with_ties, pick_all, 0)
    mask = jnp.where(valid, picked, 0.0) > 0.5

    ct = jnp.right_shift(col, 7) - (m * tq) // LANES
    reps = s // LANES
    q, k, v = q_ref[...], k_ref[...], v_ref[...]
    outs = []
    for h in range(GROUP_HEADS):
        sl = slice(h * HEAD_DIM, (h + 1) * HEAD_DIM)
        t0 = jnp.concatenate([t5_ref[h, 0]] * reps, axis=1)
        t1 = jnp.concatenate([t5_ref[h, 1]] * reps, axis=1)
        t2 = jnp.concatenate([t5_ref[h, 2]] * reps, axis=1)
        bias = jnp.where(ct == 0, t0, jnp.where(ct == -1, t1, t2))
        lg = _dot_nt(q[:, sl], k[:, sl]) * HEAD_DIM ** -0.5 + bias
        lg = jnp.where(mask, lg, NEG)
        outs.append(_softmax_pv(lg, v[:, sl]))
    o_ref[...] = jnp.concatenate(outs, axis=1).astype(o_ref.dtype)


def _dsa_attn(qkv3, iq3, ik_t, iw3, t5_tiles, tq=LANES):
    b, s, _ = qkv3.shape
    gw = GROUP_WIDTH
    n_sel = min(TOPK_MAX, s // 4)
    return pl.pallas_call(
        functools.partial(_dsa_kernel, tq=tq, n_sel=n_sel),
        grid=(b, s // tq),
        in_specs=[pl.BlockSpec((None, tq, gw), lambda bi, i: (bi, i, 0)),
                  pl.BlockSpec((None, s, gw), lambda bi, i: (bi, 0, 1)),
                  pl.BlockSpec((None, s, gw), lambda bi, i: (bi, 0, 2)),
                  pl.BlockSpec((None, tq, IDX_HEADS * IDX_DIM), lambda bi, i: (bi, i, 0)),
                  pl.BlockSpec((None, IDX_DIM, s), lambda bi, i: (bi, 0, 0)),
                  pl.BlockSpec((None, tq, IDX_HEADS), lambda bi, i: (bi, i, 0)),
                  pl.BlockSpec(t5_tiles.shape, lambda bi, i: (0, 0, 0, 0))],
        out_specs=pl.BlockSpec((None, tq, gw), lambda bi, i: (bi, i, 0)),
        out_shape=jax.ShapeDtypeStruct((b, s, gw), BF16),
        scratch_shapes=[pltpu.VMEM((tq, s), F32)],
        compiler_params=_params("parallel", "parallel"),
        name="dsa_attn",
    )(qkv3, qkv3, qkv3, iq3, ik_t, iw3, t5_tiles)


def _t5_bucket(rel):
    half = T5_BUCKETS // 2
    max_exact = half // 2
    n = jnp.abs(rel)
    large = max_exact + (jnp.log(jnp.maximum(n, 1).astype(F32) / max_exact)
                         / math.log(T5_MAX_DIST / max_exact) * (half - max_exact)).astype(jnp.int32)
    large = jnp.minimum(large, half - 1)
    return jnp.where(rel > 0, half, 0) + jnp.where(n < max_exact, n, large)


def _t5_tiles(t5_rel_bias):
    i = np.arange(LANES)[:, None]
    j = np.arange(LANES)[None, :]
    rel = np.stack([j - i, j - i - LANES, np.full((LANES, LANES), -2 * LANES)]).astype(np.int32)
    tiles = t5_rel_bias[_t5_bucket(jnp.asarray(rel))]
    return jnp.moveaxis(tiles, -1, 0).astype(F32)


def _mla_prep_kernel(misc_ref, qn_ref, kvn_ref, wqa_ref, wqb_ref, wk_ref, wv_ref, ea_ref, eb_ref,
                     cos_ref, sin_ref, q_ref, k_ref, v_ref):
    misc = misc_ref[...]
    cq = misc[:, CQ_OFF:CQ_OFF + 256]
    ckv = misc[:, CKV_OFF:CKV_OFF + KV_LORA]
    small = misc[:, SMALL_OFF:SMALL_OFF + 128]
    cqn = cq * lax.rsqrt(jnp.sum(cq * cq, axis=1, keepdims=True) / Q_LORA + RMS_EPS) * qn_ref[...]
    ckvn = ckv * lax.rsqrt(jnp.mean(ckv * ckv, axis=1, keepdims=True) + RMS_EPS) * kvn_ref[...]
    cqb, ckvb = cqn.astype(BF16), ckvn.astype(BF16)
    cos, sin = cos_ref[...], sin_ref[...]
    q = _dot(cqb, wqa_ref[...]) * cos + _dot(cqb, wqb_ref[...]) * sin
    k = (_dot(ckvb, wk_ref[...]) + _dot_f32(small, ea_ref[...])) * cos + _dot_f32(small, eb_ref[...]) * sin
    q_ref[...] = q.astype(BF16)
    k_ref[...] = k.astype(BF16)
    v_ref[...] = _dot(ckvb, wv_ref[...]).astype(BF16)


def _mla_prep(misc, qn, kvn, wqa, wqb, wk, wv, ea, eb, cos_t, sin_t, s, tm=512):
    n = misc.shape[0]
    pos_blocks = s // tm
    full = lambda a: pl.BlockSpec(a.shape, lambda i: (0,) * a.ndim)
    hw = GROUP_HEADS * LANES
    return pl.pallas_call(
        _mla_prep_kernel,
        grid=(n // tm,),
        in_specs=[pl.BlockSpec((tm, MISC_W), lambda i: (i, 0)),
                  full(qn), full(kvn), full(wqa), full(wqb), full(wk), full(wv), full(ea), full(eb),
                  pl.BlockSpec((tm, hw), lambda i: (i % pos_blocks, 0)),
                  pl.BlockSpec((tm, hw), lambda i: (i % pos_blocks, 0))],
        out_specs=[pl.BlockSpec((tm, hw), lambda i: (i, 0)),
                   pl.BlockSpec((tm, hw), lambda i: (i, 0)),
                   pl.BlockSpec((tm, GROUP_WIDTH), lambda i: (i, 0))],
        out_shape=[jax.ShapeDtypeStruct((n, hw), BF16),
                   jax.ShapeDtypeStruct((n, hw), BF16),
                   jax.ShapeDtypeStruct((n, GROUP_WIDTH), BF16)],
        compiler_params=_params("parallel"),
        name="mla_prep",
    )(misc, qn, kvn, wqa, wqb, wk, wv, ea, eb, cos_t, sin_t)


def _mla_kernel(q_ref, k_ref, v_ref, o_ref, *, tq):
    i = pl.program_id(1)
    q, k, v = q_ref[...], k_ref[...], v_ref[...]
    s = k.shape[0]
    row = i * tq + lax.broadcasted_iota(jnp.int32, (tq, s), 0)
    col = lax.broadcasted_iota(jnp.int32, (tq, s), 1)
    ok = col < _chunk_end(row)
    outs = []
    for h in range(GROUP_HEADS):
        hl = slice(h * LANES, (h + 1) * LANES)
        lg = _dot_nt(q[:, hl], k[:, hl]) * (MLA_NOPE + MLA_ROPE) ** -0.5
        lg = jnp.where(ok, lg, NEG)
        outs.append(_softmax_pv(lg, v[:, h * HEAD_DIM:(h + 1) * HEAD_DIM]))
    o_ref[...] = jnp.concatenate(outs, axis=1).astype(o_ref.dtype)


def _mla_attn(q3, k3, v3, tq=256):
    b, s, hw = q3.shape
    gw = GROUP_WIDTH
    return pl.pallas_call(
        functools.partial(_mla_kernel, tq=tq),
        grid=(b, s // tq),
        in_specs=[pl.BlockSpec((None, tq, hw), lambda bi, i: (bi, i, 0)),
                  pl.BlockSpec((None, s, hw), lambda bi, i: (bi, 0, 0)),
                  pl.BlockSpec((None, s, gw), lambda bi, i: (bi, 0, 0))],
        out_specs=pl.BlockSpec((None, tq, gw), lambda bi, i: (bi, i, 0)),
        out_shape=jax.ShapeDtypeStruct((b, s, gw), BF16),
        compiler_params=_params("parallel", "parallel"),
        name="mla_attn",
    )(q3, k3, v3)


def _mla_weights(w_uq, w_ukv):
    half = MLA_ROPE // 2
    hw = GROUP_HEADS * LANES
    wqa = jnp.zeros((256, hw), F32)
    wqb = jnp.zeros((256, hw), F32)
    wk = jnp.zeros((KV_LORA, hw), F32)
    wv = jnp.zeros((KV_LORA, GROUP_WIDTH), F32)
    for h in range(GROUP_HEADS):
        qh = w_uq[:, h * (MLA_NOPE + MLA_ROPE):(h + 1) * (MLA_NOPE + MLA_ROPE)]
        nope, r1, r2 = qh[:, :MLA_NOPE], qh[:, MLA_NOPE:MLA_NOPE + half], qh[:, MLA_NOPE + half:]
        o = h * LANES
        wqa = wqa.at[:Q_LORA, o:o + MLA_NOPE].set(nope)
        wqa = wqa.at[:Q_LORA, o + MLA_NOPE:o + MLA_NOPE + half].set(r1)
        wqa = wqa.at[:Q_LORA, o + MLA_NOPE + half:o + MLA_NOPE + 2 * half].set(r2)
        wqb = wqb.at[:Q_LORA, o + MLA_NOPE:o + MLA_NOPE + half].set(r2)
        wqb = wqb.at[:Q_LORA, o + MLA_NOPE + half:o + MLA_NOPE + 2 * half].set(r1)
        kvh = w_ukv[:, h * 2 * HEAD_DIM:(h + 1) * 2 * HEAD_DIM]
        wk = wk.at[:, o:o + MLA_NOPE].set(kvh[:, :MLA_NOPE])
        wv = wv.at[:, h * HEAD_DIM:(h + 1) * HEAD_DIM].set(kvh[:, MLA_NOPE:])
    return wqa.astype(BF16), wqb.astype(BF16), wk.astype(BF16), wv.astype(BF16)


def _mla_tables(s):
    half = MLA_ROPE // 2
    hw = GROUP_HEADS * LANES
    inv_freq = ROPE_THETA ** (-jnp.arange(half, dtype=F32) / half)
    ang = jnp.arange(s).astype(F32)[:, None] * inv_freq[None, :]
    c, sn = jnp.cos(ang), jnp.sin(ang)
    head_cos = jnp.concatenate([jnp.ones((s, MLA_NOPE), F32), c, c, jnp.zeros((s, LANES - MLA_NOPE - 2 * half), F32)], axis=1)
    head_sin = jnp.concatenate([jnp.zeros((s, MLA_NOPE), F32), -sn, sn, jnp.zeros((s, LANES - MLA_NOPE - 2 * half), F32)], axis=1)
    cos_t = jnp.concatenate([head_cos] * GROUP_HEADS, axis=1)
    sin_t = jnp.concatenate([head_sin] * GROUP_HEADS, axis=1)
    ea = np.zeros((128, hw), np.float32)
    eb = np.zeros((128, hw), np.float32)
    kr = KR_OFF - SMALL_OFF
    for h in range(GROUP_HEADS):
        o = h * LANES + MLA_NOPE
        for j in range(half):
            ea[kr + j, o + j] = 1.0
            ea[kr + half + j, o + half + j] = 1.0
            eb[kr + half + j, o + j] = 1.0
            eb[kr + j, o + half + j] = 1.0
    return cos_t, sin_t, jnp.asarray(ea), jnp.asarray(eb)


def _band_kernel(q_ref, k_ref, v_ref, bias_ref, o_ref, *, tq, win):
    m = pl.program_id(1)
    start = pl.multiple_of(m * tq, tq)
    q = q_ref[...]
    k = k_ref[pl.ds(start, win), :]
    v = v_ref[pl.ds(start, win), :]
    pad = LEFT_CHUNKS * CHUNK
    col = lax.broadcasted_iota(jnp.int32, (tq, win), 1)
    real = col + m * tq >= pad
    outs = []
    for h in range(GROUP_HEADS):
        sl = slice(h * HEAD_DIM, (h + 1) * HEAD_DIM)
        lg = _dot_nt(q[:, sl], k[:, sl]) * HEAD_DIM ** -0.5 + bias_ref[h]
        lg = jnp.where(real, lg, NEG)
        outs.append(_softmax_pv(lg, v[:, sl]))
    o_ref[...] = jnp.concatenate(outs, axis=1).astype(o_ref.dtype)


def _band_attn(qkv_pad, bias, tq=LANES):
    b, sp, _ = qkv_pad.shape
    pad = LEFT_CHUNKS * CHUNK
    s = sp - pad
    gw = GROUP_WIDTH
    win = pad + tq
    return pl.pallas_call(
        functools.partial(_band_kernel, tq=tq, win=win),
        grid=(b, s // tq),
        in_specs=[pl.BlockSpec((None, tq, gw), lambda bi, i: (bi, i + pad // tq, 0)),
                  pl.BlockSpec((None, sp, gw), lambda bi, i: (bi, 0, 1)),
                  pl.BlockSpec((None, sp, gw), lambda bi, i: (bi, 0, 2)),
                  pl.BlockSpec(bias.shape, lambda bi, i: (0, 0, 0))],
        out_specs=pl.BlockSpec((None, tq, gw), lambda bi, i: (bi, i, 0)),
        out_shape=jax.ShapeDtypeStruct((b, s, gw), BF16),
        compiler_params=_params("parallel", "parallel"),
        name="band_attn",
    )(qkv_pad, qkv_pad, qkv_pad, bias)


def _band_bias(rel_table, tq=LANES):
    pad = LEFT_CHUNKS * CHUNK
    win = pad + tq
    i = np.arange(tq)[:, None]
    r = np.arange(win)[None, :]
    dist = i + pad - r
    kc = r // CHUNK - LEFT_CHUNKS
    qc = i // CHUNK
    in_band = (kc <= qc) & (kc >= qc - LEFT_CHUNKS)
    bias = rel_table[:, np.clip(dist, -MAX_REL, MAX_REL) + MAX_REL].astype(F32)
    return jnp.where(jnp.asarray(in_band)[None], bias, NEG)


def _outproj_kernel(a_ref, b_ref, c_ref, d_ref, w_ref, x_ref, g_ref, bt_ref, wr_ref, rb_ref,
                    y_ref, gate_ref, *, alpha):
    acc = alpha * x_ref[...]
    for j, ref in enumerate((a_ref, b_ref, c_ref, d_ref)):
        acc = acc + _dot(ref[...], w_ref[j * GROUP_WIDTH:(j + 1) * GROUP_WIDTH, :])
    y = _layer_norm(acc, g_ref[...], bt_ref[...])
    y_ref[...] = y

    score = 1.0 / (1.0 + jnp.exp(-_dot_f32(y, wr_ref[...])))
    pick = score + rb_ref[...]
    lane = lax.broadcasted_iota(jnp.int32, score.shape, 1).astype(F32)
    raw = jnp.zeros_like(score)
    for _ in range(TOP_K):
        best = jnp.max(pick, axis=1, keepdims=True)
        first = jnp.min(jnp.where(pick == best, lane, float(LANES)), axis=1, keepdims=True)
        hit = lane == first
        raw = jnp.where(hit, score, raw)
        pick = jnp.where(hit, -jnp.inf, pick)
    gate_ref[...] = raw / jnp.sum(raw, axis=1, keepdims=True) * ROUTED_SCALE


def _outproj_ln_router(mix, w_out, x2, g, bt, w_router, router_bias, alpha, tm=512):
    n = x2.shape[0]
    full = lambda a: pl.BlockSpec(a.shape, lambda i: (0,) * a.ndim)
    return pl.pallas_call(
        functools.partial(_outproj_kernel, alpha=alpha),
        grid=(n // tm,),
        in_specs=[pl.BlockSpec((tm, GROUP_WIDTH), lambda i: (i, 0))] * 4
                 + [full(w_out), pl.BlockSpec((tm, D_MODEL), lambda i: (i, 0)),
                    full(g), full(bt), full(w_router), full(router_bias)],
        out_specs=[pl.BlockSpec((tm, D_MODEL), lambda i: (i, 0)),
                   pl.BlockSpec((tm, LANES), lambda i: (i, 0))],
        out_shape=[jax.ShapeDtypeStruct((n, D_MODEL), F32),
                   jax.ShapeDtypeStruct((n, LANES), F32)],
        compiler_params=_params("parallel"),
        name="outproj_ln_router",
    )(*mix, w_out, x2, g, bt, w_router, router_bias)


def _moe_kernel(x_ref, gate_ref, wgu_ref, wd_ref, wsgu_ref, wsd_ref, g_ref, bt_ref, o_ref,
                acc_ref, xb_ref, *, alpha):
    e = pl.program_id(1)

    @pl.when(e == 0)
    def _():
        xb = x_ref[...].astype(BF16)
        xb_ref[...] = xb
        hs = _dot(xb, wsgu_ref[...].astype(BF16))
        a = _silu(hs[:, :D_SHARED]) * hs[:, D_SHARED:]
        acc_ref[...] = _dot(a.astype(BF16), wsd_ref[...].astype(BF16))

    h = _dot(xb_ref[...], wgu_ref[...].astype(BF16))
    gates = gate_ref[...]
    lane = lax.broadcasted_iota(jnp.int32, gates.shape, 1)
    ge = jnp.sum(jnp.where(lane == e, gates, 0.0), axis=1, keepdims=True)
    a = _silu(h[:, :D_EXPERT]) * h[:, D_EXPERT:] * ge
    acc_ref[...] += _dot(a.astype(BF16), wd_ref[...].astype(BF16))

    @pl.when(e == pl.num_programs(1) - 1)
    def _():
        y = alpha * x_ref[...] + acc_ref[...]
        o_ref[...] = _layer_norm(y, g_ref[...], bt_ref[...])


def _moe_ln(x2, gates, w_gu, w_down, w_sh_gu, w_sh_down, g, bt, layer, alpha, tm=1024):
    n = x2.shape[0]
    full = lambda a: pl.BlockSpec(a.shape, lambda i, e: (0,) * a.ndim)
    return pl.pallas_call(
        functools.partial(_moe_kernel, alpha=alpha),
        grid=(n // tm, N_EXPERTS),
        in_specs=[pl.BlockSpec((tm, D_MODEL), lambda i, e: (i, 0)),
                  pl.BlockSpec((tm, LANES), lambda i, e: (i, 0)),
                  pl.BlockSpec((None, None, D_MODEL, 2 * D_EXPERT), lambda i, e: (layer, e, 0, 0)),
                  pl.BlockSpec((None, None, D_EXPERT, D_MODEL), lambda i, e: (layer, e, 0, 0)),
                  full(w_sh_gu), full(w_sh_down), full(g), full(bt)],
        out_specs=pl.BlockSpec((tm, D_MODEL), lambda i, e: (i, 0)),
        out_shape=jax.ShapeDtypeStruct((n, D_MODEL), F32),
        scratch_shapes=[pltpu.VMEM((tm, D_MODEL), F32), pltpu.VMEM((tm, D_MODEL), BF16)],
        compiler_params=_params("parallel", "arbitrary"),
        name="moe_ln",
    )(x2, gates, w_gu, w_down, w_sh_gu, w_sh_down, g, bt)


def _inproj_weights(w_in):
    sizes = (3 * GROUP_WIDTH, GROUP_HEADS, 3 * GROUP_WIDTH, IDX_HEADS * IDX_DIM, IDX_DIM, IDX_HEADS,
             Q_LORA, KV_LORA, MLA_ROPE, 3 * GROUP_WIDTH)
    splits = tuple(int(v) for v in np.cumsum(sizes)[:-1])
    (fox, fox_f, dsa, idx_q, idx_k, idx_w, c_q, c_kv, k_r, band) = jnp.split(w_in, splits, axis=-1)
    lead = w_in.shape[:-1]
    misc = jnp.zeros(lead + (MISC_W,), w_in.dtype)
    for off, part in ((CQ_OFF, c_q), (CKV_OFF, c_kv), (FF_OFF, fox_f), (IW_OFF, idx_w), (IK_OFF, idx_k), (KR_OFF, k_r)):
        misc = misc.at[..., off:off + part.shape[-1]].set(part)
    return jnp.concatenate([fox, dsa, band, idx_q, misc], axis=-1).astype(BF16)


def _pad_lanes(a, width, value=0.0):
    return jnp.pad(a, [(0, 0)] * (a.ndim - 1) + [(0, width - a.shape[-1])], constant_values=value)


def kernel(x, w_in, b_forget, mla_q_norm, mla_kv_norm, w_mla_uq, w_mla_ukv, t5_rel_bias, chunk_rel_bias,
           w_out, ln1_g, ln1_b, w_router, router_bias, w_exp_gu, w_exp_down, w_sh_gu, w_sh_down, ln2_g, ln2_b):
    b, s, d = x.shape
    depth = w_in.shape[0]
    n = b * s
    alpha = (2 * depth) ** 0.25

    w_in_re = _inproj_weights(w_in)
    t5_tiles = _t5_tiles(t5_rel_bias)
    cos_t, sin_t, ea, eb = _mla_tables(s)
    w_out_b = w_out.astype(BF16)
    w_router_p = _pad_lanes(w_router, LANES)
    router_bias_p = _pad_lanes(router_bias, LANES, -jnp.inf)

    x2 = x.reshape(n, d)
    for l in range(depth):
        fox, dsa, band, iq, misc = _inproj(x2, w_in_re[l])
        misc3 = misc.reshape(b, s, MISC_W)

        f_t = jnp.transpose(misc3[:, :, FF_OFF:FF_OFF + 8], (0, 2, 1))
        cum_row = _fox_cum(f_t, _pad_lanes(b_forget[l][None, :], 8).T)
        cum_col = jnp.transpose(cum_row, (0, 2, 1))
        o_fox = _fox_attn(fox.reshape(b, s, -1), cum_col, cum_row)

        ik_t = jnp.transpose(misc3[:, :, IK_OFF:IK_OFF + IDX_DIM], (0, 2, 1)).astype(BF16)
        iw3 = misc3[:, :, IW_OFF:IW_OFF + IDX_HEADS]
        o_dsa = _dsa_attn(dsa.reshape(b, s, -1), iq.reshape(b, s, -1), ik_t, iw3, t5_tiles)

        wqa, wqb, wk, wv = _mla_weights(w_mla_uq[l], w_mla_ukv[l])
        q_m, k_m, v_m = _mla_prep(misc, _pad_lanes(mla_q_norm[l][None, :], 256), mla_kv_norm[l][None, :],
                                  wqa, wqb, wk, wv, ea, eb, cos_t, sin_t, s)
        o_mla = _mla_attn(q_m.reshape(b, s, -1), k_m.reshape(b, s, -1), v_m.reshape(b, s, -1))

        band_pad = jnp.pad(band.reshape(b, s, -1), ((0, 0), (LEFT_CHUNKS * CHUNK, 0), (0, 0)))
        o_band = _band_attn(band_pad, _band_bias(chunk_rel_bias[l]))

        mix = [o.reshape(n, GROUP_WIDTH) for o in (o_fox, o_dsa, o_mla, o_band)]
        x2, gates = _outproj_ln_router(mix, w_out_b[l], x2, ln1_g[l][None, :], ln1_b[l][None, :],
                                       w_router_p[l], router_bias_p[l][None, :], alpha)
        x2 = _moe_ln(x2, gates, w_exp_gu, w_exp_down, w_sh_gu[l], w_sh_down[l],
                     ln2_g[l][None, :], ln2_b[l][None, :], l, alpha)
    return x2.reshape(b, s, d)
```

```python
import functools
import math

import numpy as np
import jax
import jax.numpy as jnp
from jax import lax
from jax.experimental import pallas as pl
from jax.experimental.pallas import tpu as pltpu

D_MODEL = 1024
CHUNK = 64
HEAD_DIM = 64
GROUP_HEADS = 4
GROUP_WIDTH = GROUP_HEADS * HEAD_DIM
IDX_HEADS = 8
IDX_DIM = 32
TOPK_MAX = 256
T5_BUCKETS = 32
T5_MAX_DIST = 128
Q_LORA = 192
KV_LORA = 128
MLA_NOPE = 64
MLA_ROPE = 32
ROPE_THETA = 10000.0
LEFT_CHUNKS = 8
MAX_REL = 256
N_EXPERTS = 64
TOP_K = 8
D_EXPERT = 256
D_SHARED = 256
ROUTED_SCALE = 2.5
LN_EPS = 1e-5
RMS_EPS = 1e-6

LANES = 128
NEG = -1e30
COARSE_STEPS = 10
REFINE_STEPS = 8
VMEM_LIMIT = 56 * 1024 * 1024

MISC_W = 512
CQ_OFF, CKV_OFF, SMALL_OFF = 0, 256, 384
FF_OFF, IW_OFF, IK_OFF, KR_OFF = 384, 392, 416, 448

BF16 = jnp.bfloat16
F32 = jnp.float32


def _dot(a, b):
    return jnp.dot(a, b, preferred_element_type=F32)


def _dot_nt(a, b):
    return lax.dot_general(a, b, (((1,), (1,)), ((), ())), preferred_element_type=F32)


def _dot_f32(a, b):
    return jnp.dot(a, b, preferred_element_type=F32, precision=lax.Precision.HIGHEST)


def _params(*sem):
    return pltpu.CompilerParams(dimension_semantics=sem, vmem_limit_bytes=VMEM_LIMIT)


def _layer_norm(y, g, b):
    mu = jnp.mean(y, axis=-1, keepdims=True)
    yc = y - mu
    var = jnp.mean(yc * yc, axis=-1, keepdims=True)
    return yc * lax.rsqrt(var + LN_EPS) * g + b


def _chunk_end(pos):
    return (jnp.right_shift(pos, CHUNK.bit_length() - 1) + 1) * CHUNK


def _silu(x):
    return x / (1.0 + jnp.exp(-x))


def _softmax_pv(s, v):
    m = jnp.max(s, axis=1, keepdims=True)
    p = jnp.exp(s - m)
    l = jnp.sum(p, axis=1, keepdims=True)
    return _dot(p.astype(BF16), v) / l


def _inproj_kernel(x_ref, w_ref, fox_ref, dsa_ref, band_ref, iq_ref, misc_ref):
    xb = x_ref[...].astype(BF16)
    off = 0
    for ref in (fox_ref, dsa_ref, band_ref, iq_ref, misc_ref):
        w = ref.shape[1]
        ref[...] = _dot(xb, w_ref[:, off:off + w]).astype(ref.dtype)
        off += w


def _inproj(x2, w_re, tm=512):
    n = x2.shape[0]
    widths = (3 * GROUP_WIDTH, 3 * GROUP_WIDTH, 3 * GROUP_WIDTH, IDX_HEADS * IDX_DIM, MISC_W)
    dtypes = (BF16, BF16, BF16, BF16, F32)
    return pl.pallas_call(
        _inproj_kernel,
        grid=(n // tm,),
        in_specs=[pl.BlockSpec((tm, D_MODEL), lambda i: (i, 0)),
                  pl.BlockSpec(w_re.shape, lambda i: (0, 0))],
        out_specs=[pl.BlockSpec((tm, w), lambda i: (i, 0)) for w in widths],
        out_shape=[jax.ShapeDtypeStruct((n, w), d) for w, d in zip(widths, dtypes)],
        compiler_params=_params("parallel"),
        name="inproj",
    )(x2, w_re)


def _cum_kernel(f_ref, b_ref, o_ref):
    x = f_ref[...] + b_ref[...]
    acc = jnp.minimum(x, 0.0) - jnp.log(1.0 + jnp.exp(-jnp.abs(x)))
    s = x.shape[1]
    lane = lax.broadcasted_iota(jnp.int32, x.shape, 1)
    d = 1
    while d < s:
        acc = acc + jnp.where(lane >= d, pltpu.roll(acc, d, 1), 0.0)
        d *= 2
    o_ref[...] = acc


def _fox_cum(f_t, b_col):
    b, r, s = f_t.shape
    return pl.pallas_call(
        _cum_kernel,
        grid=(b,),
        in_specs=[pl.BlockSpec((None, r, s), lambda i: (i, 0, 0)),
                  pl.BlockSpec((r, 1), lambda i: (0, 0))],
        out_specs=pl.BlockSpec((None, r, s), lambda i: (i, 0, 0)),
        out_shape=jax.ShapeDtypeStruct((b, r, s), F32),
        compiler_params=_params("parallel"),
        name="fox_cum",
    )(f_t, b_col)


def _fox_kernel(q_ref, k_ref, v_ref, cq_ref, ck_ref, o_ref, *, tq):
    i = pl.program_id(1)
    q, k, v = q_ref[...], k_ref[...], v_ref[...]
    s = k.shape[0]
    row = i * tq + lax.broadcasted_iota(jnp.int32, (tq, s), 0)
    col = lax.broadcasted_iota(jnp.int32, (tq, s), 1)
    causal = col <= row
    cq, ck = cq_ref[...], ck_ref[...]
    outs = []
    for h in range(GROUP_HEADS):
        sl = slice(h * HEAD_DIM, (h + 1) * HEAD_DIM)
        lg = _dot_nt(q[:, sl], k[:, sl]) * HEAD_DIM ** -0.5
        lg = lg + (cq[:, h:h + 1] - ck[h:h + 1, :])
        lg = jnp.where(causal, lg, NEG)
        outs.append(_softmax_pv(lg, v[:, sl]))
    o_ref[...] = jnp.concatenate(outs, axis=1).astype(o_ref.dtype)


def _fox_attn(qkv3, cum_col, cum_row, tq=256):
    b, s, _ = qkv3.shape
    gw = GROUP_WIDTH
    return pl.pallas_call(
        functools.partial(_fox_kernel, tq=tq),
        grid=(b, s // tq),
        in_specs=[pl.BlockSpec((None, tq, gw), lambda bi, i: (bi, i, 0)),
                  pl.BlockSpec((None, s, gw), lambda bi, i: (bi, 0, 1)),
                  pl.BlockSpec((None, s, gw), lambda bi, i: (bi, 0, 2)),
                  pl.BlockSpec((None, tq, 8), lambda bi, i: (bi, i, 0)),
                  pl.BlockSpec((None, 8, s), lambda bi, i: (bi, 0, 0))],
        out_specs=pl.BlockSpec((None, tq, gw), lambda bi, i: (bi, i, 0)),
        out_shape=jax.ShapeDtypeStruct((b, s, gw), BF16),
        compiler_params=_params("parallel", "parallel"),
        name="fox_attn",
    )(qkv3, qkv3, qkv3, cum_col, cum_row)


def _kth_largest(sc_ref, w, lo, hi, done, kf):
    def count_ge(t):
        return jnp.sum(jnp.where(sc_ref[:, :w] >= t, 1.0, 0.0), axis=1, keepdims=True)

    def halve(lo, hi, c_lo, c_hi):
        mid = lo + (hi - lo) * 0.5
        c = count_ge(mid)
        more = c >= kf
        return (jnp.where(more, mid, lo), jnp.where(more, hi, mid),
                jnp.where(more, c, c_lo), jnp.where(more, c_hi, c))

    def any_row(flag):
        return jnp.max(jnp.where(flag, 1.0, 0.0)) > 0.0

    c_lo, c_hi = count_ge(lo), jnp.zeros_like(lo)
    for _ in range(COARSE_STEPS):
        lo, hi, c_lo, c_hi = halve(lo, hi, c_lo, c_hi)

    def crowded(state):
        lo, hi, c_lo, c_hi, it = state
        mid = lo + (hi - lo) * 0.5
        splittable = (mid > lo) & (mid < hi)
        return jnp.logical_and(any_row(splittable & (c_lo - c_hi > 2.0) & jnp.logical_not(done)),
                               it < REFINE_STEPS)

    def refine(state):
        lo, hi, c_lo, c_hi, it = state
        return halve(lo, hi, c_lo, c_hi) + (it + 1,)

    lo, hi, c_lo, c_hi, _ = lax.while_loop(crowded, refine, (lo, hi, c_lo, c_hi, jnp.int32(0)))

    def unresolved(state):
        hi, thr, open_, it = state
        return jnp.logical_and(jnp.max(open_) > 0.0, it <= w)

    def peel(state):
        hi, thr, open_, it = state
        sc = sc_ref[:, :w]
        cand = jnp.max(jnp.where(sc < hi, sc, -jnp.inf), axis=1, keepdims=True)
        found = (open_ > 0.0) & (count_ge(cand) >= kf)
        thr = jnp.where(found, cand, thr)
        open_ = jnp.where(found, 0.0, open_)
        return jnp.where(open_ > 0.0, cand, hi), thr, open_, it + 1

    open0 = jnp.where(done, 0.0, 1.0)
    _, thr, _, _ = lax.while_loop(unresolved, peel, (hi, jnp.full_like(lo, -jnp.inf), open0, jnp.int32(0)))
    return thr


def _dsa_tile(w, q_ref, k_ref, v_ref, iq_ref, ik_ref, iw_ref, t5_ref, o_ref, sc_ref, *, tq, n_sel, front):
    m = pl.program_id(1)
    first_key = (m + 1) * tq - w
    start = pl.multiple_of(first_key + front, LANES)
    limit = _chunk_end(m * tq + lax.broadcasted_iota(jnp.int32, (tq, 1), 0))
    key = first_key + lax.broadcasted_iota(jnp.int32, (tq, w), 1)
    valid = (key >= 0) & (key < limit)

    iq = iq_ref[...]
    ik = ik_ref[pl.ds(start, w), :]
    iw = iw_ref[...] * (IDX_HEADS ** -0.5 * IDX_DIM ** -0.5)
    score = jnp.zeros((tq, w), F32)
    for h in range(IDX_HEADS):
        d = _dot_nt(iq[:, h * IDX_DIM:(h + 1) * IDX_DIM], ik)
        score = score + jnp.maximum(d, 0.0) * iw[:, h:h + 1]
    score = jnp.where(valid, score, -jnp.inf)
    sc_ref[:, :w] = score

    searching = limit > n_sel
    top = jnp.max(score, axis=1, keepdims=True)
    lo0 = jnp.where(searching, jnp.min(jnp.where(valid, score, jnp.inf), axis=1, keepdims=True), 0.0)
    hi0 = jnp.where(searching, top + jnp.abs(top) * 2.0 ** -20 + 1e-37, 0.0)
    thr = _kth_largest(sc_ref, w, lo0, hi0, jnp.logical_not(searching), float(n_sel))

    score = sc_ref[:, :w]
    gt = score > thr
    eq = score == thr
    n_gt = jnp.sum(jnp.where(gt, 1.0, 0.0), axis=1, keepdims=True)
    n_eq = jnp.sum(jnp.where(eq, 1.0, 0.0), axis=1, keepdims=True)
    room = n_sel - n_gt
    tie_split = jnp.max(jnp.where(searching & (n_eq > room), 1.0, 0.0)) > 0.0

    def pick_with_ties(_):
        lane = lax.broadcasted_iota(jnp.int32, (tq, w), 1)
        e = jnp.where(eq, 1.0, 0.0)
        inc = e
        d = 1
        while d < w:
            inc = inc + jnp.where(lane >= d, pltpu.roll(inc, d, 1), 0.0)
            d *= 2
        keep_eq = jnp.where(inc - e < room, e, 0.0)
        return jnp.where(gt, 1.0, keep_eq)

    def pick_all(_):
        return jnp.where(gt, 1.0, jnp.where(eq, 1.0, 0.0))

    picked = lax.cond(tie_split, pick_with_ties, pick_all, 0)
    mask = jnp.where(valid, picked, 0.0) > 0.5

    near = 2 * tq
    qs = q_ref[...] * HEAD_DIM ** -0.5
    k = k_ref[pl.ds(start, w), :]
    v = v_ref[pl.ds(start, w), :]
    outs = []
    for h in range(GROUP_HEADS):
        sl = slice(h * HEAD_DIM, (h + 1) * HEAD_DIM)
        lg = _dot_nt(qs[:, sl], k[:, sl])
        biased = lg[:, w - near:] + t5_ref[h]
        lg = biased if w == near else jnp.concatenate([lg[:, :w - near], biased], axis=1)
        lg = jnp.where(mask, lg, NEG)
        outs.append(_softmax_pv(lg, v[:, sl]))
    o_ref[...] = jnp.concatenate(outs, axis=1).astype(o_ref.dtype)


def _dsa_kernel(*refs, tq, n_sel, bucket, front):
    m = pl.program_id(1)
    s = refs[-1].shape[1]
    per = bucket // tq
    for j in range(s // bucket):
        pl.when(m // per == j)(functools.partial(
            _dsa_tile, (j + 1) * bucket, *refs, tq=tq, n_sel=n_sel, front=front))


def _dsa_attn(qkv_pad, iq3, ik_pad, iw3, t5_near, front, bucket, tq=LANES):
    b, sp, _ = qkv_pad.shape
    s = sp - front
    gw = GROUP_WIDTH
    n_sel = min(TOPK_MAX, s // 4)
    return pl.pallas_call(
        functools.partial(_dsa_kernel, tq=tq, n_sel=n_sel, bucket=bucket, front=front),
        grid=(b, s // tq),
        in_specs=[pl.BlockSpec((None, tq, gw), lambda bi, i: (bi, i + front // tq, 0)),
                  pl.BlockSpec((None, sp, gw), lambda bi, i: (bi, 0, 1)),
                  pl.BlockSpec((None, sp, gw), lambda bi, i: (bi, 0, 2)),
                  pl.BlockSpec((None, tq, IDX_HEADS * IDX_DIM), lambda bi, i: (bi, i, 0)),
                  pl.BlockSpec((None, sp, IDX_DIM), lambda bi, i: (bi, 0, 0)),
                  pl.BlockSpec((None, tq, IDX_HEADS), lambda bi, i: (bi, i, 0)),
                  pl.BlockSpec(t5_near.shape, lambda bi, i: (0, 0, 0))],
        out_specs=pl.BlockSpec((None, tq, gw), lambda bi, i: (bi, i, 0)),
        out_shape=jax.ShapeDtypeStruct((b, s, gw), BF16),
        scratch_shapes=[pltpu.VMEM((tq, s), F32)],
        compiler_params=_params("parallel", "parallel"),
        name="dsa_attn",
    )(qkv_pad, qkv_pad, qkv_pad, iq3, ik_pad, iw3, t5_near)


def _dsa_window(s, tq=LANES):
    bucket = max(s // 4, 2 * tq)
    return bucket, bucket - tq


def _toeplitz(w, rows, cols):
    length = rows + cols - 1
    wp = jnp.pad(w, ((0, 0), (0, 1)))
    flat = jnp.tile(wp, (1, rows))[:, :rows * length]
    return flat.reshape(w.shape[0], rows, length)[:, :, rows - 1:rows - 1 + cols]


def _t5_bucket(rel):
    half = T5_BUCKETS // 2
    max_exact = half // 2
    n = jnp.abs(rel)
    large = max_exact + (jnp.log(jnp.maximum(n, 1).astype(F32) / max_exact)
                         / math.log(T5_MAX_DIST / max_exact) * (half - max_exact)).astype(jnp.int32)
    large = jnp.minimum(large, half - 1)
    return jnp.where(rel > 0, half, 0) + jnp.where(n < max_exact, n, large)


def _t5_near(t5_rel_bias, tq=LANES):
    assert tq >= T5_MAX_DIST
    rel = np.arange(3 * tq - 1, dtype=np.int32) - (2 * tq - 1)
    vals = t5_rel_bias[_t5_bucket(jnp.asarray(rel))]
    far = t5_rel_bias[T5_BUCKETS // 2 - 1]
    return _toeplitz((vals - far[None, :]).T.astype(F32), tq, 2 * tq)


def _mla_prep_kernel(misc_ref, qn_ref, kvn_ref, wqa_ref, wqb_ref, wk_ref, wv_ref, ea_ref, eb_ref,
                     cos_ref, sin_ref, q_ref, k_ref, v_ref):
    misc = misc_ref[...]
    cq = misc[:, CQ_OFF:CQ_OFF + 256]
    ckv = misc[:, CKV_OFF:CKV_OFF + KV_LORA]
    small = misc[:, SMALL_OFF:SMALL_OFF + 128]
    cqn = cq * lax.rsqrt(jnp.sum(cq * cq, axis=1, keepdims=True) / Q_LORA + RMS_EPS) * qn_ref[...]
    ckvn = ckv * lax.rsqrt(jnp.mean(ckv * ckv, axis=1, keepdims=True) + RMS_EPS) * kvn_ref[...]
    cqb, ckvb = cqn.astype(BF16), ckvn.astype(BF16)
    cos, sin = cos_ref[...], sin_ref[...]
    q = _dot(cqb, wqa_ref[...]) * cos + _dot(cqb, wqb_ref[...]) * sin
    k = (_dot(ckvb, wk_ref[...]) + _dot_f32(small, ea_ref[...])) * cos + _dot_f32(small, eb_ref[...]) * sin
    q_ref[...] = q.astype(BF16)
    k_ref[...] = k.astype(BF16)
    v_ref[...] = _dot(ckvb, wv_ref[...]).astype(BF16)


def _mla_prep(misc, qn, kvn, wqa, wqb, wk, wv, ea, eb, cos_t, sin_t, s, tm=512):
    n = misc.shape[0]
    pos_blocks = s // tm
    full = lambda a: pl.BlockSpec(a.shape, lambda i: (0,) * a.ndim)
    hw = GROUP_HEADS * LANES
    return pl.pallas_call(
        _mla_prep_kernel,
        grid=(n // tm,),
        in_specs=[pl.BlockSpec((tm, MISC_W), lambda i: (i, 0)),
                  full(qn), full(kvn), full(wqa), full(wqb), full(wk), full(wv), full(ea), full(eb),
                  pl.BlockSpec((tm, hw), lambda i: (i % pos_blocks, 0)),
                  pl.BlockSpec((tm, hw), lambda i: (i % pos_blocks, 0))],
        out_specs=[pl.BlockSpec((tm, hw), lambda i: (i, 0)),
                   pl.BlockSpec((tm, hw), lambda i: (i, 0)),
                   pl.BlockSpec((tm, GROUP_WIDTH), lambda i: (i, 0))],
        out_shape=[jax.ShapeDtypeStruct((n, hw), BF16),
                   jax.ShapeDtypeStruct((n, hw), BF16),
                   jax.ShapeDtypeStruct((n, GROUP_WIDTH), BF16)],
        compiler_params=_params("parallel"),
        name="mla_prep",
    )(misc, qn, kvn, wqa, wqb, wk, wv, ea, eb, cos_t, sin_t)


def _mla_kernel(q_ref, k_ref, v_ref, o_ref, *, tq):
    i = pl.program_id(1)
    q, k, v = q_ref[...], k_ref[...], v_ref[...]
    s = k.shape[0]
    row = i * tq + lax.broadcasted_iota(jnp.int32, (tq, s), 0)
    col = lax.broadcasted_iota(jnp.int32, (tq, s), 1)
    ok = col < _chunk_end(row)
    outs = []
    for h in range(GROUP_HEADS):
        hl = slice(h * LANES, (h + 1) * LANES)
        lg = _dot_nt(q[:, hl], k[:, hl]) * (MLA_NOPE + MLA_ROPE) ** -0.5
        lg = jnp.where(ok, lg, NEG)
        outs.append(_softmax_pv(lg, v[:, h * HEAD_DIM:(h + 1) * HEAD_DIM]))
    o_ref[...] = jnp.concatenate(outs, axis=1).astype(o_ref.dtype)


def _mla_attn(q3, k3, v3, tq=256):
    b, s, hw = q3.shape
    gw = GROUP_WIDTH
    return pl.pallas_call(
        functools.partial(_mla_kernel, tq=tq),
        grid=(b, s // tq),
        in_specs=[pl.BlockSpec((None, tq, hw), lambda bi, i: (bi, i, 0)),
                  pl.BlockSpec((None, s, hw), lambda bi, i: (bi, 0, 0)),
                  pl.BlockSpec((None, s, gw), lambda bi, i: (bi, 0, 0))],
        out_specs=pl.BlockSpec((None, tq, gw), lambda bi, i: (bi, i, 0)),
        out_shape=jax.ShapeDtypeStruct((b, s, gw), BF16),
        compiler_params=_params("parallel", "parallel"),
        name="mla_attn",
    )(q3, k3, v3)


def _mla_weights(w_uq, w_ukv):
    half = MLA_ROPE // 2
    hw = GROUP_HEADS * LANES
    wqa = jnp.zeros((256, hw), F32)
    wqb = jnp.zeros((256, hw), F32)
    wk = jnp.zeros((KV_LORA, hw), F32)
    wv = jnp.zeros((KV_LORA, GROUP_WIDTH), F32)
    for h in range(GROUP_HEADS):
        qh = w_uq[:, h * (MLA_NOPE + MLA_ROPE):(h + 1) * (MLA_NOPE + MLA_ROPE)]
        nope, r1, r2 = qh[:, :MLA_NOPE], qh[:, MLA_NOPE:MLA_NOPE + half], qh[:, MLA_NOPE + half:]
        o = h * LANES
        wqa = wqa.at[:Q_LORA, o:o + MLA_NOPE].set(nope)
        wqa = wqa.at[:Q_LORA, o + MLA_NOPE:o + MLA_NOPE + half].set(r1)
        wqa = wqa.at[:Q_LORA, o + MLA_NOPE + half:o + MLA_NOPE + 2 * half].set(r2)
        wqb = wqb.at[:Q_LORA, o + MLA_NOPE:o + MLA_NOPE + half].set(r2)
        wqb = wqb.at[:Q_LORA, o + MLA_NOPE + half:o + MLA_NOPE + 2 * half].set(r1)
        kvh = w_ukv[:, h * 2 * HEAD_DIM:(h + 1) * 2 * HEAD_DIM]
        wk = wk.at[:, o:o + MLA_NOPE].set(kvh[:, :MLA_NOPE])
        wv = wv.at[:, h * HEAD_DIM:(h + 1) * HEAD_DIM].set(kvh[:, MLA_NOPE:])
    return wqa.astype(BF16), wqb.astype(BF16), wk.astype(BF16), wv.astype(BF16)


def _mla_tables(s):
    half = MLA_ROPE // 2
    hw = GROUP_HEADS * LANES
    inv_freq = ROPE_THETA ** (-jnp.arange(half, dtype=F32) / half)
    ang = jnp.arange(s).astype(F32)[:, None] * inv_freq[None, :]
    c, sn = jnp.cos(ang), jnp.sin(ang)
    head_cos = jnp.concatenate([jnp.ones((s, MLA_NOPE), F32), c, c, jnp.zeros((s, LANES - MLA_NOPE - 2 * half), F32)], axis=1)
    head_sin = jnp.concatenate([jnp.zeros((s, MLA_NOPE), F32), -sn, sn, jnp.zeros((s, LANES - MLA_NOPE - 2 * half), F32)], axis=1)
    cos_t = jnp.concatenate([head_cos] * GROUP_HEADS, axis=1)
    sin_t = jnp.concatenate([head_sin] * GROUP_HEADS, axis=1)
    ea = np.zeros((128, hw), np.float32)
    eb = np.zeros((128, hw), np.float32)
    kr = KR_OFF - SMALL_OFF
    for h in range(GROUP_HEADS):
        o = h * LANES + MLA_NOPE
        for j in range(half):
            ea[kr + j, o + j] = 1.0
            ea[kr + half + j, o + half + j] = 1.0
            eb[kr + half + j, o + j] = 1.0
            eb[kr + j, o + half + j] = 1.0
    return cos_t, sin_t, jnp.asarray(ea), jnp.asarray(eb)


def _band_kernel(q_ref, k_ref, v_ref, bias_ref, o_ref, *, tq, win):
    m = pl.program_id(1)
    start = pl.multiple_of(m * tq, tq)
    q = q_ref[...]
    k = k_ref[pl.ds(start, win), :]
    v = v_ref[pl.ds(start, win), :]
    pad = LEFT_CHUNKS * CHUNK
    col = lax.broadcasted_iota(jnp.int32, (tq, win), 1)
    real = col + m * tq >= pad
    outs = []
    for h in range(GROUP_HEADS):
        sl = slice(h * HEAD_DIM, (h + 1) * HEAD_DIM)
        lg = _dot_nt(q[:, sl], k[:, sl]) * HEAD_DIM ** -0.5 + bias_ref[h]
        lg = jnp.where(real, lg, NEG)
        outs.append(_softmax_pv(lg, v[:, sl]))
    o_ref[...] = jnp.concatenate(outs, axis=1).astype(o_ref.dtype)


def _band_attn(qkv_pad, bias, tq=LANES):
    b, sp, _ = qkv_pad.shape
    pad = LEFT_CHUNKS * CHUNK
    s = sp - pad
    gw = GROUP_WIDTH
    win = pad + tq
    return pl.pallas_call(
        functools.partial(_band_kernel, tq=tq, win=win),
        grid=(b, s // tq),
        in_specs=[pl.BlockSpec((None, tq, gw), lambda bi, i: (bi, i + pad // tq, 0)),
                  pl.BlockSpec((None, sp, gw), lambda bi, i: (bi, 0, 1)),
                  pl.BlockSpec((None, sp, gw), lambda bi, i: (bi, 0, 2)),
                  pl.BlockSpec(bias.shape, lambda bi, i: (0, 0, 0))],
        out_specs=pl.BlockSpec((None, tq, gw), lambda bi, i: (bi, i, 0)),
        out_shape=jax.ShapeDtypeStruct((b, s, gw), BF16),
        compiler_params=_params("parallel", "parallel"),
        name="band_attn",
    )(qkv_pad, qkv_pad, qkv_pad, bias)


def _band_bias(rel_table, tq=LANES):
    pad = LEFT_CHUNKS * CHUNK
    win = pad + tq
    i = np.arange(tq)[:, None]
    r = np.arange(win)[None, :]
    kc = r // CHUNK - LEFT_CHUNKS
    qc = i // CHUNK
    in_band = (kc <= qc) & (kc >= qc - LEFT_CHUNKS)
    diag = pad + tq - 1 - np.arange(tq + win - 1)
    bias = _toeplitz(rel_table[:, np.clip(diag, -MAX_REL, MAX_REL) + MAX_REL].astype(F32), tq, win)
    return jnp.where(jnp.asarray(in_band)[None], bias, NEG)


def _outproj_kernel(a_ref, b_ref, c_ref, d_ref, w_ref, x_ref, g_ref, bt_ref, wr_ref, rb_ref,
                    y_ref, gate_ref, *, alpha):
    acc = alpha * x_ref[...]
    for j, ref in enumerate((a_ref, b_ref, c_ref, d_ref)):
        acc = acc + _dot(ref[...], w_ref[j * GROUP_WIDTH:(j + 1) * GROUP_WIDTH, :])
    y = _layer_norm(acc, g_ref[...], bt_ref[...])
    y_ref[...] = y

    score = 1.0 / (1.0 + jnp.exp(-_dot_f32(y, wr_ref[...])))
    pick = score + rb_ref[...]
    lane = lax.broadcasted_iota(jnp.int32, score.shape, 1).astype(F32)
    raw = jnp.zeros_like(score)
    for _ in range(TOP_K):
        best = jnp.max(pick, axis=1, keepdims=True)
        first = jnp.min(jnp.where(pick == best, lane, float(LANES)), axis=1, keepdims=True)
        hit = lane == first
        raw = jnp.where(hit, score, raw)
        pick = jnp.where(hit, -jnp.inf, pick)
    gate_ref[...] = raw / jnp.sum(raw, axis=1, keepdims=True) * ROUTED_SCALE


def _outproj_ln_router(mix, w_out, x2, g, bt, w_router, router_bias, alpha, tm=512):
    n = x2.shape[0]
    full = lambda a: pl.BlockSpec(a.shape, lambda i: (0,) * a.ndim)
    return pl.pallas_call(
        functools.partial(_outproj_kernel, alpha=alpha),
        grid=(n // tm,),
        in_specs=[pl.BlockSpec((tm, GROUP_WIDTH), lambda i: (i, 0))] * 4
                 + [full(w_out), pl.BlockSpec((tm, D_MODEL), lambda i: (i, 0)),
                    full(g), full(bt), full(w_router), full(router_bias)],
        out_specs=[pl.BlockSpec((tm, D_MODEL), lambda i: (i, 0)),
                   pl.BlockSpec((tm, LANES), lambda i: (i, 0))],
        out_shape=[jax.ShapeDtypeStruct((n, D_MODEL), F32),
                   jax.ShapeDtypeStruct((n, LANES), F32)],
        compiler_params=_params("parallel"),
        name="outproj_ln_router",
    )(*mix, w_out, x2, g, bt, w_router, router_bias)


def _moe_kernel(x_ref, gate_ref, wgu_ref, wd_ref, wsgu_ref, wsd_ref, g_ref, bt_ref, o_ref,
                acc_ref, xb_ref, *, alpha):
    e = pl.program_id(1)

    @pl.when(e == 0)
    def _():
        xb = x_ref[...].astype(BF16)
        xb_ref[...] = xb
        hs = _dot(xb, wsgu_ref[...].astype(BF16))
        a = _silu(hs[:, :D_SHARED]) * hs[:, D_SHARED:]
        acc_ref[...] = _dot(a.astype(BF16), wsd_ref[...].astype(BF16))

    h = _dot(xb_ref[...], wgu_ref[...].astype(BF16))
    gates = gate_ref[...]
    lane = lax.broadcasted_iota(jnp.int32, gates.shape, 1)
    ge = jnp.sum(jnp.where(lane == e, gates, 0.0), axis=1, keepdims=True)
    a = _silu(h[:, :D_EXPERT]) * h[:, D_EXPERT:] * ge
    acc_ref[...] += _dot(a.astype(BF16), wd_ref[...].astype(BF16))

    @pl.when(e == pl.num_programs(1) - 1)
    def _():
        y = alpha * x_ref[...] + acc_ref[...]
        o_ref[...] = _layer_norm(y, g_ref[...], bt_ref[...])


def _moe_ln(x2, gates, w_gu, w_down, w_sh_gu, w_sh_down, g, bt, layer, alpha, tm=1024):
    n = x2.shape[0]
    full = lambda a: pl.BlockSpec(a.shape, lambda i, e: (0,) * a.ndim)
    return pl.pallas_call(
        functools.partial(_moe_kernel, alpha=alpha),
        grid=(n // tm, N_EXPERTS),
        in_specs=[pl.BlockSpec((tm, D_MODEL), lambda i, e: (i, 0)),
                  pl.BlockSpec((tm, LANES), lambda i, e: (i, 0)),
                  pl.BlockSpec((None, None, D_MODEL, 2 * D_EXPERT), lambda i, e: (layer, e, 0, 0)),
                  pl.BlockSpec((None, None, D_EXPERT, D_MODEL), lambda i, e: (layer, e, 0, 0)),
                  full(w_sh_gu), full(w_sh_down), full(g), full(bt)],
        out_specs=pl.BlockSpec((tm, D_MODEL), lambda i, e: (i, 0)),
        out_shape=jax.ShapeDtypeStruct((n, D_MODEL), F32),
        scratch_shapes=[pltpu.VMEM((tm, D_MODEL), F32), pltpu.VMEM((tm, D_MODEL), BF16)],
        compiler_params=_params("parallel", "arbitrary"),
        name="moe_ln",
    )(x2, gates, w_gu, w_down, w_sh_gu, w_sh_down, g, bt)


def _inproj_weights(w_in):
    sizes = (3 * GROUP_WIDTH, GROUP_HEADS, 3 * GROUP_WIDTH, IDX_HEADS * IDX_DIM, IDX_DIM, IDX_HEADS,
             Q_LORA, KV_LORA, MLA_ROPE, 3 * GROUP_WIDTH)
    splits = tuple(int(v) for v in np.cumsum(sizes)[:-1])
    (fox, fox_f, dsa, idx_q, idx_k, idx_w, c_q, c_kv, k_r, band) = jnp.split(w_in, splits, axis=-1)
    lead = w_in.shape[:-1]
    misc = jnp.zeros(lead + (MISC_W,), w_in.dtype)
    for off, part in ((CQ_OFF, c_q), (CKV_OFF, c_kv), (FF_OFF, fox_f), (IW_OFF, idx_w), (IK_OFF, idx_k), (KR_OFF, k_r)):
        misc = misc.at[..., off:off + part.shape[-1]].set(part)
    return jnp.concatenate([fox, dsa, band, idx_q, misc], axis=-1).astype(BF16)


def _pad_lanes(a, width, value=0.0):
    return jnp.pad(a, [(0, 0)] * (a.ndim - 1) + [(0, width - a.shape[-1])], constant_values=value)


def kernel(x, w_in, b_forget, mla_q_norm, mla_kv_norm, w_mla_uq, w_mla_ukv, t5_rel_bias, chunk_rel_bias,
           w_out, ln1_g, ln1_b, w_router, router_bias, w_exp_gu, w_exp_down, w_sh_gu, w_sh_down, ln2_g, ln2_b):
    b, s, d = x.shape
    depth = w_in.shape[0]
    n = b * s
    alpha = (2 * depth) ** 0.25

    w_in_re = _inproj_weights(w_in)
    t5_near = _t5_near(t5_rel_bias)
    dsa_bucket, dsa_front = _dsa_window(s)
    cos_t, sin_t, ea, eb = _mla_tables(s)
    w_out_b = w_out.astype(BF16)
    w_router_p = _pad_lanes(w_router, LANES)
    router_bias_p = _pad_lanes(router_bias, LANES, -jnp.inf)

    x2 = x.reshape(n, d)
    for l in range(depth):
        fox, dsa, band, iq, misc = _inproj(x2, w_in_re[l])
        misc3 = misc.reshape(b, s, MISC_W)

        f_t = jnp.transpose(misc3[:, :, FF_OFF:FF_OFF + 8], (0, 2, 1))
        cum_row = _fox_cum(f_t, _pad_lanes(b_forget[l][None, :], 8).T)
        cum_col = jnp.transpose(cum_row, (0, 2, 1))
        o_fox = _fox_attn(fox.reshape(b, s, -1), cum_col, cum_row)

        front_pad = ((0, 0), (dsa_front, 0), (0, 0))
        ik_pad = jnp.pad(misc3[:, :, IK_OFF:IK_OFF + IDX_DIM].astype(BF16), front_pad)
        iw3 = misc3[:, :, IW_OFF:IW_OFF + IDX_HEADS]
        o_dsa = _dsa_attn(jnp.pad(dsa.reshape(b, s, -1), front_pad), iq.reshape(b, s, -1), ik_pad, iw3,
                          t5_near, dsa_front, dsa_bucket)

        wqa, wqb, wk, wv = _mla_weights(w_mla_uq[l], w_mla_ukv[l])
        q_m, k_m, v_m = _mla_prep(misc, _pad_lanes(mla_q_norm[l][None, :], 256), mla_kv_norm[l][None, :],
                                  wqa, wqb, wk, wv, ea, eb, cos_t, sin_t, s)
        o_mla = _mla_attn(q_m.reshape(b, s, -1), k_m.reshape(b, s, -1), v_m.reshape(b, s, -1))

        band_pad = jnp.pad(band.reshape(b, s, -1), ((0, 0), (LEFT_CHUNKS * CHUNK, 0), (0, 0)))
        o_band = _band_attn(band_pad, _band_bias(chunk_rel_bias[l]))

        mix = [o.reshape(n, GROUP_WIDTH) for o in (o_fox, o_dsa, o_mla, o_band)]
        x2, gates = _outproj_ln_router(mix, w_out_b[l], x2, ln1_g[l][None, :], ln1_b[l][None, :],
                                       w_router_p[l], router_bias_p[l][None, :], alpha)
        x2 = _moe_ln(x2, gates, w_exp_gu, w_exp_down, w_sh_gu[l], w_sh_down[l],
                     ln2_g[l][None, :], ln2_b[l][None, :], l, alpha)
    return x2.reshape(b, s, d)
```

```python
import functools
import math

import numpy as np
import jax
import jax.numpy as jnp
from jax import lax
from jax.experimental import pallas as pl
from jax.experimental.pallas import tpu as pltpu

D_MODEL = 1024
CHUNK = 64
HEAD_DIM = 64
GROUP_HEADS = 4
GROUP_WIDTH = GROUP_HEADS * HEAD_DIM
IDX_HEADS = 8
IDX_DIM = 32
TOPK_MAX = 256
T5_BUCKETS = 32
T5_MAX_DIST = 128
Q_LORA = 192
KV_LORA = 128
MLA_NOPE = 64
MLA_ROPE = 32
ROPE_THETA = 10000.0
LEFT_CHUNKS = 8
MAX_REL = 256
N_EXPERTS = 64
TOP_K = 8
D_EXPERT = 256
D_SHARED = 256
ROUTED_SCALE = 2.5
MOE_GROUP = 4
LN_EPS = 1e-5
RMS_EPS = 1e-6

LANES = 128
NEG = -1e30
BISECT_STEPS = 16
VMEM_LIMIT = 56 * 1024 * 1024

MISC_W = 512
CQ_OFF, CKV_OFF, SMALL_OFF = 0, 256, 384
FF_OFF, IW_OFF, IK_OFF, KR_OFF = 384, 392, 416, 448

BF16 = jnp.bfloat16
F32 = jnp.float32


def _dot(a, b):
    return jnp.dot(a, b, preferred_element_type=F32)


def _dot_nt(a, b):
    return lax.dot_general(a, b, (((1,), (1,)), ((), ())), preferred_element_type=F32)


def _dot_f32(a, b):
    return jnp.dot(a, b, preferred_element_type=F32, precision=lax.Precision.HIGHEST)


def _params(*sem):
    return pltpu.CompilerParams(dimension_semantics=sem, vmem_limit_bytes=VMEM_LIMIT)


def _layer_norm(y, g, b):
    mu = jnp.mean(y, axis=-1, keepdims=True)
    yc = y - mu
    var = jnp.mean(yc * yc, axis=-1, keepdims=True)
    return yc * lax.rsqrt(var + LN_EPS) * g + b


def _chunk_end(pos):
    return (jnp.right_shift(pos, CHUNK.bit_length() - 1) + 1) * CHUNK


def _silu(x):
    return x / (1.0 + jnp.exp(-x))


def _softmax_pv(s, v):
    m = jnp.max(s, axis=1, keepdims=True)
    p = jnp.exp(s - m)
    l = jnp.sum(p, axis=1, keepdims=True)
    return _dot(p.astype(BF16), v) / l


def _inproj_kernel(x_ref, w_ref, fox_ref, dsa_ref, band_ref, iq_ref, misc_ref):
    xb = x_ref[...].astype(BF16)
    off = 0
    for ref in (fox_ref, dsa_ref, band_ref, iq_ref, misc_ref):
        w = ref.shape[1]
        ref[...] = _dot(xb, w_ref[:, off:off + w]).astype(ref.dtype)
        off += w


def _inproj(x2, w_re, tm=512):
    n = x2.shape[0]
    widths = (3 * GROUP_WIDTH, 3 * GROUP_WIDTH, 3 * GROUP_WIDTH, IDX_HEADS * IDX_DIM, MISC_W)
    dtypes = (BF16, BF16, BF16, BF16, F32)
    return pl.pallas_call(
        _inproj_kernel,
        grid=(n // tm,),
        in_specs=[pl.BlockSpec((tm, D_MODEL), lambda i: (i, 0)),
                  pl.BlockSpec(w_re.shape, lambda i: (0, 0))],
        out_specs=[pl.BlockSpec((tm, w), lambda i: (i, 0)) for w in widths],
        out_shape=[jax.ShapeDtypeStruct((n, w), d) for w, d in zip(widths, dtypes)],
        compiler_params=_params("parallel"),
        name="inproj",
    )(x2, w_re)


def _cum_kernel(f_ref, b_ref, o_ref):
    x = f_ref[...] + b_ref[...]
    acc = jnp.minimum(x, 0.0) - jnp.log(1.0 + jnp.exp(-jnp.abs(x)))
    s = x.shape[1]
    lane = lax.broadcasted_iota(jnp.int32, x.shape, 1)
    d = 1
    while d < s:
        acc = acc + jnp.where(lane >= d, pltpu.roll(acc, d, 1), 0.0)
        d *= 2
    o_ref[...] = acc


def _fox_cum(f_t, b_col):
    b, r, s = f_t.shape
    return pl.pallas_call(
        _cum_kernel,
        grid=(b,),
        in_specs=[pl.BlockSpec((None, r, s), lambda i: (i, 0, 0)),
                  pl.BlockSpec((r, 1), lambda i: (0, 0))],
        out_specs=pl.BlockSpec((None, r, s), lambda i: (i, 0, 0)),
        out_shape=jax.ShapeDtypeStruct((b, r, s), F32),
        compiler_params=_params("parallel"),
        name="fox_cum",
    )(f_t, b_col)


def _mask_last_block(lg, ok):
    tq = ok.shape[1]
    w = lg.shape[1]
    last = jnp.where(ok, lg[:, w - tq:], NEG)
    return last if w == tq else jnp.concatenate([lg[:, :w - tq], last], axis=1)


def _per_tile(body, n_tiles, tq):
    i = pl.program_id(1)
    for j in range(n_tiles):
        pl.when(i == j)(functools.partial(body, (j + 1) * tq))


def _fox_kernel(q_ref, k_ref, v_ref, ck_ref, o_ref, *, tq):
    def tile(w):
        q = q_ref[...] * HEAD_DIM ** -0.5
        k, v, ck = k_ref[:w, :], v_ref[:w, :], ck_ref[:, :w]
        ok = (lax.broadcasted_iota(jnp.int32, (tq, tq), 1) <= lax.broadcasted_iota(jnp.int32, (tq, tq), 0))
        outs = []
        for h in range(GROUP_HEADS):
            sl = slice(h * HEAD_DIM, (h + 1) * HEAD_DIM)
            lg = _dot_nt(q[:, sl], k[:, sl]) - ck[h:h + 1, :]
            outs.append(_softmax_pv(_mask_last_block(lg, ok), v[:, sl]))
        o_ref[...] = jnp.concatenate(outs, axis=1).astype(o_ref.dtype)

    _per_tile(tile, k_ref.shape[0] // tq, tq)


def _fox_attn(qkv3, cum_row, tq=256):
    b, s, _ = qkv3.shape
    gw = GROUP_WIDTH
    return pl.pallas_call(
        functools.partial(_fox_kernel, tq=tq),
        grid=(b, s // tq),
        in_specs=[pl.BlockSpec((None, tq, gw), lambda bi, i: (bi, i, 0)),
                  pl.BlockSpec((None, s, gw), lambda bi, i: (bi, 0, 1)),
                  pl.BlockSpec((None, s, gw), lambda bi, i: (bi, 0, 2)),
                  pl.BlockSpec((None, 8, s), lambda bi, i: (bi, 0, 0))],
        out_specs=pl.BlockSpec((None, tq, gw), lambda bi, i: (bi, i, 0)),
        out_shape=jax.ShapeDtypeStruct((b, s, gw), BF16),
        compiler_params=_params("parallel", "parallel"),
        name="fox_attn",
    )(qkv3, qkv3, qkv3, cum_row)


def _over_keys(reduce, x, chains=4):
    w, tq = x.shape
    sub = 8
    part = reduce(x.reshape(chains, w // (sub * chains), sub, tq), axis=1)
    return reduce(reduce(part, axis=0), axis=0, keepdims=True)


def _kth_largest(sc_ref, w, lo, hi, searching, kf):
    def count_ge(t):
        return _over_keys(jnp.sum, jnp.where(sc_ref[:w, :] >= t, 1.0, 0.0))

    def halve(_, bracket):
        lo, hi = bracket
        mid = lo + (hi - lo) * 0.5
        more = count_ge(mid) >= kf
        return jnp.where(more, mid, lo), jnp.where(more, hi, mid)

    _, hi = lax.fori_loop(0, BISECT_STEPS, halve, (lo, hi))

    def unresolved(state):
        hi, thr, open_, it = state
        return jnp.logical_and(jnp.max(open_) > 0.0, it <= w)

    def peel(state):
        hi, thr, open_, it = state
        sc = sc_ref[:w, :]
        cand = _over_keys(jnp.max, jnp.where(sc < hi, sc, -jnp.inf))
        found = (open_ > 0.0) & (count_ge(cand) >= kf)
        thr = jnp.where(found, cand, thr)
        open_ = jnp.where(found, 0.0, open_)
        return jnp.where(open_ > 0.0, cand, hi), thr, open_, it + 1

    open0 = jnp.where(searching, 1.0, 0.0)
    _, thr, _, _ = lax.while_loop(unresolved, peel, (hi, jnp.full_like(lo, -jnp.inf), open0, jnp.int32(0)))
    return thr


def _dsa_tile(w, q_ref, k_ref, vt_ref, iq_ref, ik_ref, iwt_ref, t5_ref, ot_ref, sc_ref, *, tq, n_sel):
    m = pl.program_id(1)
    limit = _chunk_end(m * tq + lax.broadcasted_iota(jnp.int32, (1, tq), 1))
    valid = lax.broadcasted_iota(jnp.int32, (w, tq), 0) < limit

    iq = iq_ref[...]
    ik = ik_ref[:w, :]
    iw = iwt_ref[...] * (IDX_HEADS ** -0.5 * IDX_DIM ** -0.5)
    score = jnp.zeros((w, tq), F32)
    for h in range(IDX_HEADS):
        d = _dot_nt(ik, iq[:, h * IDX_DIM:(h + 1) * IDX_DIM])
        score = score + jnp.maximum(d, 0.0) * iw[h:h + 1, :]
    score = jnp.where(valid, score, -jnp.inf)
    sc_ref[:w, :] = score

    searching = limit > n_sel
    top = _over_keys(jnp.max, score)
    lo0 = jnp.where(searching, _over_keys(jnp.min, jnp.where(valid, score, jnp.inf)), 0.0)
    hi0 = jnp.where(searching, top + jnp.abs(top) * 2.0 ** -20 + 1e-37, 0.0)
    thr = _kth_largest(sc_ref, w, lo0, hi0, searching, float(n_sel))

    score = sc_ref[:w, :]
    gt = score > thr
    eq = score == thr
    n_gt = _over_keys(jnp.sum, jnp.where(gt, 1.0, 0.0))
    n_eq = _over_keys(jnp.sum, jnp.where(eq, 1.0, 0.0))
    room = n_sel - n_gt
    tie_split = jnp.max(jnp.where(searching & (n_eq > room), 1.0, 0.0)) > 0.0

    def pick_with_ties(_):
        blk = 2 * tq
        tri = (lax.broadcasted_iota(jnp.int32, (blk, blk), 1)
               <= lax.broadcasted_iota(jnp.int32, (blk, blk), 0)).astype(BF16)
        e = jnp.where(eq, 1.0, 0.0)
        before = jnp.zeros((1, tq), F32)
        kept = []
        for c in range(w // blk):
            eb = e[c * blk:(c + 1) * blk, :]
            inc = _dot(tri, eb.astype(BF16)) + before
            kept.append(jnp.where(inc - eb < room, eb, 0.0))
            before = before + jnp.sum(eb, axis=0, keepdims=True)
        return jnp.where(gt, 1.0, jnp.concatenate(kept, axis=0))

    def pick_all(_):
        return jnp.where(gt, 1.0, jnp.where(eq, 1.0, 0.0))

    picked = lax.cond(tie_split, pick_with_ties, pick_all, 0)
    mask = jnp.where(valid, picked, 0.0) > 0.5

    near = 3 * tq
    shift = (w - (m + 1) * tq) // tq
    qs = q_ref[...] * HEAD_DIM ** -0.5
    k = k_ref[:w, :]
    outs = []
    for h in range(GROUP_HEADS):
        sl = slice(h * HEAD_DIM, (h + 1) * HEAD_DIM)
        lg = _dot_nt(k[:, sl], qs[:, sl])
        tab = t5_ref[h, shift]
        if w < near:
            lg = lg + tab[near - w:, :]
        else:
            biased = lg[w - near:, :] + tab
            lg = biased if w == near else jnp.concatenate([lg[:w - near, :], biased], axis=0)
        lg = jnp.where(mask, lg, NEG)
        p = jnp.exp(lg - _over_keys(jnp.max, lg))
        l = _over_keys(jnp.sum, p)
        outs.append(_dot(vt_ref[sl, :w], p.astype(BF16)) / l)
    ot_ref[...] = jnp.concatenate(outs, axis=0).astype(ot_ref.dtype)


def _dsa_kernel(*refs, tq, n_sel):
    m = pl.program_id(1)
    s = refs[-1].shape[0]
    bucket = 2 * tq
    for j in range(s // bucket):
        pl.when(m // 2 == j)(functools.partial(_dsa_tile, (j + 1) * bucket, *refs, tq=tq, n_sel=n_sel))


def _dsa_attn(qkv3, v_t, iq3, ik3, iw_t, t5_near, tq=LANES):
    b, s, _ = qkv3.shape
    gw = GROUP_WIDTH
    n_sel = min(TOPK_MAX, s // 4)
    return pl.pallas_call(
        functools.partial(_dsa_kernel, tq=tq, n_sel=n_sel),
        grid=(b, s // tq),
        in_specs=[pl.BlockSpec((None, tq, gw), lambda bi, i: (bi, i, 0)),
                  pl.BlockSpec((None, s, gw), lambda bi, i: (bi, 0, 1)),
                  pl.BlockSpec((None, gw, s), lambda bi, i: (bi, 0, 0)),
                  pl.BlockSpec((None, tq, IDX_HEADS * IDX_DIM), lambda bi, i: (bi, i, 0)),
                  pl.BlockSpec((None, s, IDX_DIM), lambda bi, i: (bi, 0, 0)),
                  pl.BlockSpec((None, IDX_HEADS, tq), lambda bi, i: (bi, 0, i)),
                  pl.BlockSpec(t5_near.shape, lambda bi, i: (0, 0, 0, 0))],
        out_specs=pl.BlockSpec((None, gw, tq), lambda bi, i: (bi, 0, i)),
        out_shape=jax.ShapeDtypeStruct((b, gw, s), BF16),
        scratch_shapes=[pltpu.VMEM((s, tq), F32)],
        compiler_params=_params("parallel", "parallel"),
        name="dsa_attn",
    )(qkv3, qkv3, v_t, iq3, ik3, iw_t, t5_near)


def _toeplitz(w, rows, cols):
    length = rows + cols - 1
    wp = jnp.pad(w, ((0, 0), (0, 1)))
    flat = jnp.tile(wp, (1, rows))[:, :rows * length]
    return flat.reshape(w.shape[0], rows, length)[:, :, rows - 1:rows - 1 + cols]


def _t5_bucket(rel):
    half = T5_BUCKETS // 2
    max_exact = half // 2
    n = jnp.abs(rel)
    large = max_exact + (jnp.log(jnp.maximum(n, 1).astype(F32) / max_exact)
                         / math.log(T5_MAX_DIST / max_exact) * (half - max_exact)).astype(jnp.int32)
    large = jnp.minimum(large, half - 1)
    return jnp.where(rel > 0, half, 0) + jnp.where(n < max_exact, n, large)


def _t5_near(t5_rel_bias, tq=LANES):
    assert tq >= T5_MAX_DIST
    far = t5_rel_bias[T5_BUCKETS // 2 - 1]
    tables = []
    for shift in range(2):
        rel = (1 + shift) * tq - 1 - np.arange(4 * tq - 1, dtype=np.int32)
        vals = t5_rel_bias[_t5_bucket(jnp.asarray(rel))]
        tables.append(_toeplitz((vals - far[None, :]).T.astype(F32), 3 * tq, tq))
    return jnp.stack(tables, axis=1)


def _mla_prep_kernel(misc_ref, qn_ref, kvn_ref, wqa_ref, wqb_ref, wk_ref, wv_ref, ea_ref, eb_ref,
                     cos_ref, sin_ref, q_ref, k_ref, v_ref):
    misc = misc_ref[...]
    cq = misc[:, CQ_OFF:CQ_OFF + 256]
    ckv = misc[:, CKV_OFF:CKV_OFF + KV_LORA]
    small = misc[:, SMALL_OFF:SMALL_OFF + 128]
    cqn = cq * lax.rsqrt(jnp.sum(cq * cq, axis=1, keepdims=True) / Q_LORA + RMS_EPS) * qn_ref[...]
    ckvn = ckv * lax.rsqrt(jnp.mean(ckv * ckv, axis=1, keepdims=True) + RMS_EPS) * kvn_ref[...]
    cqb, ckvb = cqn.astype(BF16), ckvn.astype(BF16)
    cos, sin = cos_ref[...], sin_ref[...]
    q = (_dot(cqb, wqa_ref[...]) * cos + _dot(cqb, wqb_ref[...]) * sin) * (MLA_NOPE + MLA_ROPE) ** -0.5
    k = (_dot(ckvb, wk_ref[...]) + _dot_f32(small, ea_ref[...])) * cos + _dot_f32(small, eb_ref[...]) * sin
    q_ref[...] = q.astype(BF16)
    k_ref[...] = k.astype(BF16)
    v_ref[...] = _dot(ckvb, wv_ref[...]).astype(BF16)


def _mla_prep(misc, qn, kvn, wqa, wqb, wk, wv, ea, eb, cos_t, sin_t, s, tm=512):
    n = misc.shape[0]
    pos_blocks = s // tm
    full = lambda a: pl.BlockSpec(a.shape, lambda i: (0,) * a.ndim)
    hw = GROUP_HEADS * LANES
    return pl.pallas_call(
        _mla_prep_kernel,
        grid=(n // tm,),
        in_specs=[pl.BlockSpec((tm, MISC_W), lambda i: (i, 0)),
                  full(qn), full(kvn), full(wqa), full(wqb), full(wk), full(wv), full(ea), full(eb),
                  pl.BlockSpec((tm, hw), lambda i: (i % pos_blocks, 0)),
                  pl.BlockSpec((tm, hw), lambda i: (i % pos_blocks, 0))],
        out_specs=[pl.BlockSpec((tm, hw), lambda i: (i, 0)),
                   pl.BlockSpec((tm, hw), lambda i: (i, 0)),
                   pl.BlockSpec((tm, GROUP_WIDTH), lambda i: (i, 0))],
        out_shape=[jax.ShapeDtypeStruct((n, hw), BF16),
                   jax.ShapeDtypeStruct((n, hw), BF16),
                   jax.ShapeDtypeStruct((n, GROUP_WIDTH), BF16)],
        compiler_params=_params("parallel"),
        name="mla_prep",
    )(misc, qn, kvn, wqa, wqb, wk, wv, ea, eb, cos_t, sin_t)


def _mla_kernel(q_ref, k_ref, v_ref, o_ref, *, tq):
    def tile(w):
        q, k, v = q_ref[...], k_ref[:w, :], v_ref[:w, :]
        row = lax.broadcasted_iota(jnp.int32, (tq, tq), 0)
        col = lax.broadcasted_iota(jnp.int32, (tq, tq), 1)
        ok = col < _chunk_end(row)
        outs = []
        for h in range(GROUP_HEADS):
            hl = slice(h * LANES, (h + 1) * LANES)
            lg = _mask_last_block(_dot_nt(q[:, hl], k[:, hl]), ok)
            outs.append(_softmax_pv(lg, v[:, h * HEAD_DIM:(h + 1) * HEAD_DIM]))
        o_ref[...] = jnp.concatenate(outs, axis=1).astype(o_ref.dtype)

    _per_tile(tile, k_ref.shape[0] // tq, tq)


def _mla_attn(q3, k3, v3, tq=256):
    b, s, hw = q3.shape
    gw = GROUP_WIDTH
    return pl.pallas_call(
        functools.partial(_mla_kernel, tq=tq),
        grid=(b, s // tq),
        in_specs=[pl.BlockSpec((None, tq, hw), lambda bi, i: (bi, i, 0)),
                  pl.BlockSpec((None, s, hw), lambda bi, i: (bi, 0, 0)),
                  pl.BlockSpec((None, s, gw), lambda bi, i: (bi, 0, 0))],
        out_specs=pl.BlockSpec((None, tq, gw), lambda bi, i: (bi, i, 0)),
        out_shape=jax.ShapeDtypeStruct((b, s, gw), BF16),
        compiler_params=_params("parallel", "parallel"),
        name="mla_attn",
    )(q3, k3, v3)


def _mla_weights(w_uq, w_ukv):
    half = MLA_ROPE // 2
    hw = GROUP_HEADS * LANES
    wqa = jnp.zeros((256, hw), F32)
    wqb = jnp.zeros((256, hw), F32)
    wk = jnp.zeros((KV_LORA, hw), F32)
    wv = jnp.zeros((KV_LORA, GROUP_WIDTH), F32)
    for h in range(GROUP_HEADS):
        qh = w_uq[:, h * (MLA_NOPE + MLA_ROPE):(h + 1) * (MLA_NOPE + MLA_ROPE)]
        nope, r1, r2 = qh[:, :MLA_NOPE], qh[:, MLA_NOPE:MLA_NOPE + half], qh[:, MLA_NOPE + half:]
        o = h * LANES
        wqa = wqa.at[:Q_LORA, o:o + MLA_NOPE].set(nope)
        wqa = wqa.at[:Q_LORA, o + MLA_NOPE:o + MLA_NOPE + half].set(r1)
        wqa = wqa.at[:Q_LORA, o + MLA_NOPE + half:o + MLA_NOPE + 2 * half].set(r2)
        wqb = wqb.at[:Q_LORA, o + MLA_NOPE:o + MLA_NOPE + half].set(r2)
        wqb = wqb.at[:Q_LORA, o + MLA_NOPE + half:o + MLA_NOPE + 2 * half].set(r1)
        kvh = w_ukv[:, h * 2 * HEAD_DIM:(h + 1) * 2 * HEAD_DIM]
        wk = wk.at[:, o:o + MLA_NOPE].set(kvh[:, :MLA_NOPE])
        wv = wv.at[:, h * HEAD_DIM:(h + 1) * HEAD_DIM].set(kvh[:, MLA_NOPE:])
    return wqa.astype(BF16), wqb.astype(BF16), wk.astype(BF16), wv.astype(BF16)


def _mla_tables(s):
    half = MLA_ROPE // 2
    hw = GROUP_HEADS * LANES
    inv_freq = ROPE_THETA ** (-jnp.arange(half, dtype=F32) / half)
    ang = jnp.arange(s).astype(F32)[:, None] * inv_freq[None, :]
    c, sn = jnp.cos(ang), jnp.sin(ang)
    head_cos = jnp.concatenate([jnp.ones((s, MLA_NOPE), F32), c, c, jnp.zeros((s, LANES - MLA_NOPE - 2 * half), F32)], axis=1)
    head_sin = jnp.concatenate([jnp.zeros((s, MLA_NOPE), F32), -sn, sn, jnp.zeros((s, LANES - MLA_NOPE - 2 * half), F32)], axis=1)
    cos_t = jnp.concatenate([head_cos] * GROUP_HEADS, axis=1)
    sin_t = jnp.concatenate([head_sin] * GROUP_HEADS, axis=1)
    ea = np.zeros((128, hw), np.float32)
    eb = np.zeros((128, hw), np.float32)
    kr = KR_OFF - SMALL_OFF
    for h in range(GROUP_HEADS):
        o = h * LANES + MLA_NOPE
        for j in range(half):
            ea[kr + j, o + j] = 1.0
            ea[kr + half + j, o + half + j] = 1.0
            eb[kr + half + j, o + j] = 1.0
            eb[kr + j, o + half + j] = 1.0
    return cos_t, sin_t, jnp.asarray(ea), jnp.asarray(eb)


def _band_kernel(q_ref, k_ref, v_ref, bias_ref, o_ref, *, tq, win):
    m = pl.program_id(1)
    start = pl.multiple_of(m * tq, tq)
    q = q_ref[...]
    k = k_ref[pl.ds(start, win), :]
    v = v_ref[pl.ds(start, win), :]
    pad = LEFT_CHUNKS * CHUNK
    col = lax.broadcasted_iota(jnp.int32, (tq, win), 1)
    real = col + m * tq >= pad
    outs = []
    for h in range(GROUP_HEADS):
        sl = slice(h * HEAD_DIM, (h + 1) * HEAD_DIM)
        lg = _dot_nt(q[:, sl], k[:, sl]) * HEAD_DIM ** -0.5 + bias_ref[h]
        lg = jnp.where(real, lg, NEG)
        outs.append(_softmax_pv(lg, v[:, sl]))
    o_ref[...] = jnp.concatenate(outs, axis=1).astype(o_ref.dtype)


def _band_attn(qkv_pad, bias, tq=LANES):
    b, sp, _ = qkv_pad.shape
    pad = LEFT_CHUNKS * CHUNK
    s = sp - pad
    gw = GROUP_WIDTH
    win = pad + tq
    return pl.pallas_call(
        functools.partial(_band_kernel, tq=tq, win=win),
        grid=(b, s // tq),
        in_specs=[pl.BlockSpec((None, tq, gw), lambda bi, i: (bi, i + pad // tq, 0)),
                  pl.BlockSpec((None, sp, gw), lambda bi, i: (bi, 0, 1)),
                  pl.BlockSpec((None, sp, gw), lambda bi, i: (bi, 0, 2)),
                  pl.BlockSpec(bias.shape, lambda bi, i: (0, 0, 0))],
        out_specs=pl.BlockSpec((None, tq, gw), lambda bi, i: (bi, i, 0)),
        out_shape=jax.ShapeDtypeStruct((b, s, gw), BF16),
        compiler_params=_params("parallel", "parallel"),
        name="band_attn",
    )(qkv_pad, qkv_pad, qkv_pad, bias)


def _band_bias(rel_table, tq=LANES):
    pad = LEFT_CHUNKS * CHUNK
    win = pad + tq
    i = np.arange(tq)[:, None]
    r = np.arange(win)[None, :]
    kc = r // CHUNK - LEFT_CHUNKS
    qc = i // CHUNK
    in_band = (kc <= qc) & (kc >= qc - LEFT_CHUNKS)
    diag = pad + tq - 1 - np.arange(tq + win - 1)
    bias = _toeplitz(rel_table[:, np.clip(diag, -MAX_REL, MAX_REL) + MAX_REL].astype(F32), tq, win)
    return jnp.where(jnp.asarray(in_band)[None], bias, NEG)


def _outproj_kernel(a_ref, b_ref, c_ref, d_ref, w_ref, x_ref, g_ref, bt_ref, wr_ref, rb_ref,
                    y_ref, gate_ref, *, alpha):
    acc = alpha * x_ref[...]
    for j, ref in enumerate((a_ref, b_ref, c_ref, d_ref)):
        acc = acc + _dot(ref[...], w_ref[j * GROUP_WIDTH:(j + 1) * GROUP_WIDTH, :])
    y = _layer_norm(acc, g_ref[...], bt_ref[...])
    y_ref[...] = y

    score = 1.0 / (1.0 + jnp.exp(-_dot_f32(y, wr_ref[...])))
    pick = score + rb_ref[...]
    lane = lax.broadcasted_iota(jnp.int32, score.shape, 1).astype(F32)
    raw = jnp.zeros_like(score)
    for _ in range(TOP_K):
        best = jnp.max(pick, axis=1, keepdims=True)
        first = jnp.min(jnp.where(pick == best, lane, float(LANES)), axis=1, keepdims=True)
        hit = lane == first
        raw = jnp.where(hit, score, raw)
        pick = jnp.where(hit, -jnp.inf, pick)
    gate_ref[...] = raw / jnp.sum(raw, axis=1, keepdims=True) * ROUTED_SCALE


def _outproj_ln_router(mix, w_out, x2, g, bt, w_router, router_bias, alpha, tm=512):
    n = x2.shape[0]
    full = lambda a: pl.BlockSpec(a.shape, lambda i: (0,) * a.ndim)
    return pl.pallas_call(
        functools.partial(_outproj_kernel, alpha=alpha),
        grid=(n // tm,),
        in_specs=[pl.BlockSpec((tm, GROUP_WIDTH), lambda i: (i, 0))] * 4
                 + [full(w_out), pl.BlockSpec((tm, D_MODEL), lambda i: (i, 0)),
                    full(g), full(bt), full(w_router), full(router_bias)],
        out_specs=[pl.BlockSpec((tm, D_MODEL), lambda i: (i, 0)),
                   pl.BlockSpec((tm, LANES), lambda i: (i, 0))],
        out_shape=[jax.ShapeDtypeStruct((n, D_MODEL), F32),
                   jax.ShapeDtypeStruct((n, LANES), F32)],
        compiler_params=_params("parallel"),
        name="outproj_ln_router",
    )(*mix, w_out, x2, g, bt, w_router, router_bias)


def _moe_kernel(x_ref, gate_ref, wgu_ref, wd_ref, wsgu_ref, wsd_ref, g_ref, bt_ref, o_ref,
                acc_ref, xb_ref, *, alpha, group):
    e = pl.program_id(1)

    @pl.when(e == 0)
    def _():
        xb = x_ref[...].astype(BF16)
        xb_ref[...] = xb
        hs = _dot(xb, wsgu_ref[...])
        a = _silu(hs[:, :D_SHARED]) * hs[:, D_SHARED:]
        acc_ref[...] = _dot(a.astype(BF16), wsd_ref[...])

    xb = xb_ref[...]
    gates = gate_ref[...]
    acts = []
    for j in range(group):
        h = _dot(xb, wgu_ref[j])
        a = _silu(h[:, :D_EXPERT]) * h[:, D_EXPERT:] * gates[:, j:j + 1]
        acts.append(a.astype(BF16))
    w_down = wd_ref[...].reshape(group * D_EXPERT, D_MODEL)
    acc_ref[...] += _dot(jnp.concatenate(acts, axis=1), w_down)

    @pl.when(e == pl.num_programs(1) - 1)
    def _():
        y = alpha * x_ref[...] + acc_ref[...]
        o_ref[...] = _layer_norm(y, g_ref[...], bt_ref[...])


def _moe_ln(x2, gates_g, w_gu, w_down, w_sh_gu, w_sh_down, g, bt, layer, alpha, tm=1024):
    n = x2.shape[0]
    n_groups, _, group = gates_g.shape
    full = lambda a: pl.BlockSpec(a.shape, lambda i, e: (0,) * a.ndim)
    return pl.pallas_call(
        functools.partial(_moe_kernel, alpha=alpha, group=group),
        grid=(n // tm, n_groups),
        in_specs=[pl.BlockSpec((tm, D_MODEL), lambda i, e: (i, 0)),
                  pl.BlockSpec((None, tm, group), lambda i, e: (e, i, 0)),
                  pl.BlockSpec((None, group, D_MODEL, 2 * D_EXPERT), lambda i, e: (layer, e, 0, 0)),
                  pl.BlockSpec((None, group, D_EXPERT, D_MODEL), lambda i, e: (layer, e, 0, 0)),
                  full(w_sh_gu), full(w_sh_down), full(g), full(bt)],
        out_specs=pl.BlockSpec((tm, D_MODEL), lambda i, e: (i, 0)),
        out_shape=jax.ShapeDtypeStruct((n, D_MODEL), F32),
        scratch_shapes=[pltpu.VMEM((tm, D_MODEL), F32), pltpu.VMEM((tm, D_MODEL), BF16)],
        compiler_params=_params("parallel", "arbitrary"),
        name="moe_ln",
    )(x2, gates_g, w_gu, w_down, w_sh_gu, w_sh_down, g, bt)


def _inproj_weights(w_in):
    sizes = (3 * GROUP_WIDTH, GROUP_HEADS, 3 * GROUP_WIDTH, IDX_HEADS * IDX_DIM, IDX_DIM, IDX_HEADS,
             Q_LORA, KV_LORA, MLA_ROPE, 3 * GROUP_WIDTH)
    splits = tuple(int(v) for v in np.cumsum(sizes)[:-1])
    (fox, fox_f, dsa, idx_q, idx_k, idx_w, c_q, c_kv, k_r, band) = jnp.split(w_in, splits, axis=-1)
    lead = w_in.shape[:-1]
    misc = jnp.zeros(lead + (MISC_W,), w_in.dtype)
    for off, part in ((CQ_OFF, c_q), (CKV_OFF, c_kv), (FF_OFF, fox_f), (IW_OFF, idx_w), (IK_OFF, idx_k), (KR_OFF, k_r)):
        misc = misc.at[..., off:off + part.shape[-1]].set(part)
    return jnp.concatenate([fox, dsa, band, idx_q, misc], axis=-1).astype(BF16)


def _pad_lanes(a, width, value=0.0):
    return jnp.pad(a, [(0, 0)] * (a.ndim - 1) + [(0, width - a.shape[-1])], constant_values=value)


def kernel(x, w_in, b_forget, mla_q_norm, mla_kv_norm, w_mla_uq, w_mla_ukv, t5_rel_bias, chunk_rel_bias,
           w_out, ln1_g, ln1_b, w_router, router_bias, w_exp_gu, w_exp_down, w_sh_gu, w_sh_down, ln2_g, ln2_b):
    b, s, d = x.shape
    depth = w_in.shape[0]
    n = b * s
    alpha = (2 * depth) ** 0.25

    w_in_re = _inproj_weights(w_in)
    t5_near = _t5_near(t5_rel_bias)
    cos_t, sin_t, ea, eb = _mla_tables(s)
    w_out_b = w_out.astype(BF16)
    w_gu_b, w_down_b = w_exp_gu.astype(BF16), w_exp_down.astype(BF16)
    w_sh_gu_b, w_sh_down_b = w_sh_gu.astype(BF16), w_sh_down.astype(BF16)
    w_router_p = _pad_lanes(w_router, LANES)
    router_bias_p = _pad_lanes(router_bias, LANES, -jnp.inf)

    x2 = x.reshape(n, d)
    for l in range(depth):
        fox, dsa, band, iq, misc = _inproj(x2, w_in_re[l])
        misc3 = misc.reshape(b, s, MISC_W)

        f_t = jnp.transpose(misc3[:, :, FF_OFF:FF_OFF + 8], (0, 2, 1))
        cum_row = _fox_cum(f_t, _pad_lanes(b_forget[l][None, :], 8).T)
        o_fox = _fox_attn(fox.reshape(b, s, -1), cum_row)

        dsa3 = dsa.reshape(b, s, -1)
        v_t = jnp.transpose(dsa3[:, :, 2 * GROUP_WIDTH:], (0, 2, 1))
        ik3 = misc3[:, :, IK_OFF:IK_OFF + IDX_DIM].astype(BF16)
        iw_t = jnp.transpose(misc3[:, :, IW_OFF:IW_OFF + IDX_HEADS], (0, 2, 1))
        o_dsa = jnp.transpose(_dsa_attn(dsa3, v_t, iq.reshape(b, s, -1), ik3, iw_t, t5_near), (0, 2, 1))

        wqa, wqb, wk, wv = _mla_weights(w_mla_uq[l], w_mla_ukv[l])
        q_m, k_m, v_m = _mla_prep(misc, _pad_lanes(mla_q_norm[l][None, :], 256), mla_kv_norm[l][None, :],
                                  wqa, wqb, wk, wv, ea, eb, cos_t, sin_t, s)
        o_mla = _mla_attn(q_m.reshape(b, s, -1), k_m.reshape(b, s, -1), v_m.reshape(b, s, -1))

        band_pad = jnp.pad(band.reshape(b, s, -1), ((0, 0), (LEFT_CHUNKS * CHUNK, 0), (0, 0)))
        o_band = _band_attn(band_pad, _band_bias(chunk_rel_bias[l]))

        mix = [o.reshape(n, GROUP_WIDTH) for o in (o_fox, o_dsa, o_mla, o_band)]
        x2, gates = _outproj_ln_router(mix, w_out_b[l], x2, ln1_g[l][None, :], ln1_b[l][None, :],
                                       w_router_p[l], router_bias_p[l][None, :], alpha)
        gates_g = jnp.transpose(gates[:, :N_EXPERTS].reshape(n, N_EXPERTS // MOE_GROUP, MOE_GROUP), (1, 0, 2))
        x2 = _moe_ln(x2, gates_g, w_gu_b, w_down_b, w_sh_gu_b[l], w_sh_down_b[l],
                     ln2_g[l][None, :], ln2_b[l][None, :], l, alpha)
    return x2.reshape(b, s, d)
```

```python
import functools
import math

import numpy as np
import jax
import jax.numpy as jnp
from jax import lax
from jax.experimental import pallas as pl
from jax.experimental.pallas import tpu as pltpu

D_MODEL = 1024
CHUNK = 64
HEAD_DIM = 64
GROUP_HEADS = 4
GROUP_WIDTH = GROUP_HEADS * HEAD_DIM
IDX_HEADS = 8
IDX_DIM = 32
TOPK_MAX = 256
T5_BUCKETS = 32
T5_MAX_DIST = 128
Q_LORA = 192
KV_LORA = 128
MLA_NOPE = 64
MLA_ROPE = 32
ROPE_THETA = 10000.0
LEFT_CHUNKS = 8
MAX_REL = 256
N_EXPERTS = 64
TOP_K = 8
D_EXPERT = 256
D_SHARED = 256
ROUTED_SCALE = 2.5
MOE_GROUP = 4
BAND_TQ = 256
LN_EPS = 1e-5
RMS_EPS = 1e-6

LANES = 128
NEG = -1e30
BISECT_STEPS = 16
VMEM_LIMIT = 56 * 1024 * 1024

MISC_W = 512
CQ_OFF, CKV_OFF, SMALL_OFF = 0, 256, 384
FF_OFF, IW_OFF, IK_OFF, KR_OFF = 384, 392, 416, 448
SMALL_T_ROWS = 16

BF16 = jnp.bfloat16
F32 = jnp.float32


def _dot(a, b):
    return jnp.dot(a, b, preferred_element_type=F32)


def _dot_nt(a, b):
    return lax.dot_general(a, b, (((1,), (1,)), ((), ())), preferred_element_type=F32)


def _params(*sem):
    return pltpu.CompilerParams(dimension_semantics=sem, vmem_limit_bytes=VMEM_LIMIT)


def _layer_norm(y, g, b):
    mu = jnp.mean(y, axis=-1, keepdims=True)
    yc = y - mu
    var = jnp.mean(yc * yc, axis=-1, keepdims=True)
    return yc * lax.rsqrt(var + LN_EPS) * g + b


def _chunk_end(pos):
    return (jnp.right_shift(pos, CHUNK.bit_length() - 1) + 1) * CHUNK


def _silu(x):
    return x / (1.0 + jnp.exp(-x))


def _softmax_pv(s, v):
    m = jnp.max(s, axis=1, keepdims=True)
    p = jnp.exp(s - m)
    l = jnp.sum(p, axis=1, keepdims=True)
    return _dot(p.astype(BF16), v) / l


def _inproj_kernel(x_ref, w_ref, wt_ref, fox_ref, dsa_ref, band_ref, iq_ref, misc_ref, vt_ref, small_t_ref):
    xb = x_ref[...].astype(BF16)
    off = 0
    for ref in (fox_ref, dsa_ref, band_ref, iq_ref, misc_ref):
        w = ref.shape[1]
        ref[...] = _dot(xb, w_ref[:, off:off + w]).astype(ref.dtype)
        off += w
    t = _dot_nt(wt_ref[...], xb)
    vt_ref[...] = t[:GROUP_WIDTH, :].astype(vt_ref.dtype)
    small_t_ref[...] = t[GROUP_WIDTH:, :]


def _inproj(x2, w_re, w_t, s, tm=512):
    n = x2.shape[0]
    per_seq = s // tm
    widths = (3 * GROUP_WIDTH, 2 * GROUP_WIDTH, 3 * GROUP_WIDTH, IDX_HEADS * IDX_DIM, MISC_W)
    dtypes = (BF16, BF16, BF16, BF16, F32)
    t_rows = (GROUP_WIDTH, SMALL_T_ROWS)
    t_dtypes = (BF16, F32)
    return pl.pallas_call(
        _inproj_kernel,
        grid=(n // tm,),
        in_specs=[pl.BlockSpec((tm, D_MODEL), lambda i: (i, 0)),
                  pl.BlockSpec(w_re.shape, lambda i: (0, 0)),
                  pl.BlockSpec(w_t.shape, lambda i: (0, 0))],
        out_specs=[pl.BlockSpec((tm, w), lambda i: (i, 0)) for w in widths]
                  + [pl.BlockSpec((None, r, tm), lambda i: (i // per_seq, 0, i % per_seq)) for r in t_rows],
        out_shape=[jax.ShapeDtypeStruct((n, w), d) for w, d in zip(widths, dtypes)]
                  + [jax.ShapeDtypeStruct((n // s, r, s), d) for r, d in zip(t_rows, t_dtypes)],
        compiler_params=_params("parallel"),
        name="inproj",
    )(x2, w_re, w_t)


def _cum_kernel(f_ref, b_ref, o_ref):
    x = f_ref[...] + b_ref[...]
    acc = jnp.minimum(x, 0.0) - jnp.log(1.0 + jnp.exp(-jnp.abs(x)))
    s = x.shape[1]
    lane = lax.broadcasted_iota(jnp.int32, x.shape, 1)
    d = 1
    while d < s:
        acc = acc + jnp.where(lane >= d, pltpu.roll(acc, d, 1), 0.0)
        d *= 2
    o_ref[...] = acc


def _fox_cum(f_t, b_col):
    b, _, s = f_t.shape
    r = b_col.shape[0]
    return pl.pallas_call(
        _cum_kernel,
        grid=(b,),
        in_specs=[pl.BlockSpec((None, r, s), lambda i: (i, 0, 0)),
                  pl.BlockSpec((r, 1), lambda i: (0, 0))],
        out_specs=pl.BlockSpec((None, r, s), lambda i: (i, 0, 0)),
        out_shape=jax.ShapeDtypeStruct((b, r, s), F32),
        compiler_params=_params("parallel"),
        name="fox_cum",
    )(f_t, b_col)


def _mask_last_block(lg, ok):
    tq = ok.shape[1]
    w = lg.shape[1]
    last = jnp.where(ok, lg[:, w - tq:], NEG)
    return last if w == tq else jnp.concatenate([lg[:, :w - tq], last], axis=1)


def _per_tile(body, n_tiles, tq):
    i = pl.program_id(1)
    for j in range(n_tiles):
        pl.when(i == j)(functools.partial(body, (j + 1) * tq))


def _fox_kernel(q_ref, k_ref, v_ref, ck_ref, o_ref, *, tq):
    def tile(w):
        q = q_ref[...] * HEAD_DIM ** -0.5
        k, v, ck = k_ref[:w, :], v_ref[:w, :], ck_ref[:, :w]
        ok = (lax.broadcasted_iota(jnp.int32, (tq, tq), 1) <= lax.broadcasted_iota(jnp.int32, (tq, tq), 0))
        outs = []
        for h in range(GROUP_HEADS):
            sl = slice(h * HEAD_DIM, (h + 1) * HEAD_DIM)
            lg = _dot_nt(q[:, sl], k[:, sl]) - ck[h:h + 1, :]
            outs.append(_softmax_pv(_mask_last_block(lg, ok), v[:, sl]))
        o_ref[...] = jnp.concatenate(outs, axis=1).astype(o_ref.dtype)

    _per_tile(tile, k_ref.shape[0] // tq, tq)


def _fox_attn(qkv3, cum_row, tq=256):
    b, s, _ = qkv3.shape
    gw = GROUP_WIDTH
    return pl.pallas_call(
        functools.partial(_fox_kernel, tq=tq),
        grid=(b, s // tq),
        in_specs=[pl.BlockSpec((None, tq, gw), lambda bi, i: (bi, i, 0)),
                  pl.BlockSpec((None, s, gw), lambda bi, i: (bi, 0, 1)),
                  pl.BlockSpec((None, s, gw), lambda bi, i: (bi, 0, 2)),
                  pl.BlockSpec((None, 8, s), lambda bi, i: (bi, 0, 0))],
        out_specs=pl.BlockSpec((None, tq, gw), lambda bi, i: (bi, i, 0)),
        out_shape=jax.ShapeDtypeStruct((b, s, gw), BF16),
        compiler_params=_params("parallel", "parallel"),
        name="fox_attn",
    )(qkv3, qkv3, qkv3, cum_row)


def _over_keys(reduce, x, chains=4):
    w, tq = x.shape
    sub = 8
    part = reduce(x.reshape(chains, w // (sub * chains), sub, tq), axis=1)
    return reduce(reduce(part, axis=0), axis=0, keepdims=True)


def _kth_largest(sc_ref, w, lo, hi, searching, kf):
    def count_ge(t):
        return _over_keys(jnp.sum, jnp.where(sc_ref[:w, :] >= t, 1.0, 0.0))

    def halve(_, bracket):
        lo, hi = bracket
        mid = lo + (hi - lo) * 0.5
        more = count_ge(mid) >= kf
        return jnp.where(more, mid, lo), jnp.where(more, hi, mid)

    _, hi = lax.fori_loop(0, BISECT_STEPS, halve, (lo, hi))

    def unresolved(state):
        hi, thr, open_, it = state
        return jnp.logical_and(jnp.max(open_) > 0.0, it <= w)

    def peel(state):
        hi, thr, open_, it = state
        sc = sc_ref[:w, :]
        cand = _over_keys(jnp.max, jnp.where(sc < hi, sc, -jnp.inf))
        found = (open_ > 0.0) & (count_ge(cand) >= kf)
        thr = jnp.where(found, cand, thr)
        open_ = jnp.where(found, 0.0, open_)
        return jnp.where(open_ > 0.0, cand, hi), thr, open_, it + 1

    open0 = jnp.where(searching, 1.0, 0.0)
    _, thr, _, _ = lax.while_loop(unresolved, peel, (hi, jnp.full_like(lo, -jnp.inf), open0, jnp.int32(0)))
    return thr


def _dsa_tile(w, q_ref, k_ref, vt_ref, iq_ref, ik_ref, iwt_ref, t5_ref, o_ref, sc_ref, *, tq, n_sel):
    m = pl.program_id(1)
    limit = _chunk_end(m * tq + lax.broadcasted_iota(jnp.int32, (1, tq), 1))
    valid = lax.broadcasted_iota(jnp.int32, (w, tq), 0) < limit

    iq = iq_ref[...]
    ik = ik_ref[:w, :]
    iw = iwt_ref[...] * (IDX_HEADS ** -0.5 * IDX_DIM ** -0.5)
    score = jnp.zeros((w, tq), F32)
    for h in range(IDX_HEADS):
        d = _dot_nt(ik, iq[:, h * IDX_DIM:(h + 1) * IDX_DIM])
        score = score + jnp.maximum(d, 0.0) * iw[h:h + 1, :]
    score = jnp.where(valid, score, -jnp.inf)
    sc_ref[:w, :] = score

    searching = limit > n_sel
    top = _over_keys(jnp.max, score)
    lo0 = jnp.where(searching, _over_keys(jnp.min, jnp.where(valid, score, jnp.inf)), 0.0)
    hi0 = jnp.where(searching, top + jnp.abs(top) * 2.0 ** -20 + 1e-37, 0.0)
    thr = _kth_largest(sc_ref, w, lo0, hi0, searching, float(n_sel))

    score = sc_ref[:w, :]
    gt = score > thr
    eq = score == thr
    n_gt = _over_keys(jnp.sum, jnp.where(gt, 1.0, 0.0))
    n_eq = _over_keys(jnp.sum, jnp.where(eq, 1.0, 0.0))
    room = n_sel - n_gt
    tie_split = jnp.max(jnp.where(searching & (n_eq > room), 1.0, 0.0)) > 0.0

    def pick_with_ties(_):
        blk = 2 * tq
        tri = (lax.broadcasted_iota(jnp.int32, (blk, blk), 1)
               <= lax.broadcasted_iota(jnp.int32, (blk, blk), 0)).astype(BF16)
        e = jnp.where(eq, 1.0, 0.0)
        before = jnp.zeros((1, tq), F32)
        kept = []
        for c in range(w // blk):
            eb = e[c * blk:(c + 1) * blk, :]
            inc = _dot(tri, eb.astype(BF16)) + before
            kept.append(jnp.where(inc - eb < room, eb, 0.0))
            before = before + jnp.sum(eb, axis=0, keepdims=True)
        return jnp.where(gt, 1.0, jnp.concatenate(kept, axis=0))

    def pick_all(_):
        return jnp.where(gt, 1.0, jnp.where(eq, 1.0, 0.0))

    picked = lax.cond(tie_split, pick_with_ties, pick_all, 0)
    mask = jnp.where(valid, picked, 0.0) > 0.5

    near = 3 * tq
    shift = (w - (m + 1) * tq) // tq
    qs = q_ref[...] * HEAD_DIM ** -0.5
    k = k_ref[:w, :]
    outs = []
    for h in range(GROUP_HEADS):
        sl = slice(h * HEAD_DIM, (h + 1) * HEAD_DIM)
        lg = _dot_nt(k[:, sl], qs[:, sl])
        tab = t5_ref[h, shift]
        if w < near:
            lg = lg + tab[near - w:, :]
        else:
            biased = lg[w - near:, :] + tab
            lg = biased if w == near else jnp.concatenate([lg[:w - near, :], biased], axis=0)
        lg = jnp.where(mask, lg, NEG)
        p = jnp.exp(lg - _over_keys(jnp.max, lg))
        l = _over_keys(jnp.sum, p)
        outs.append(_dot(vt_ref[sl, :w], p.astype(BF16)) / l)
    o_ref[...] = jnp.concatenate(outs, axis=0).T.astype(o_ref.dtype)


def _dsa_kernel(*refs, tq, n_sel):
    m = pl.program_id(1)
    s = refs[-1].shape[0]
    bucket = 2 * tq
    for j in range(s // bucket):
        pl.when(m // 2 == j)(functools.partial(_dsa_tile, (j + 1) * bucket, *refs, tq=tq, n_sel=n_sel))


def _dsa_attn(qk3, v_t, iq3, ik3, small_t, t5_near, tq=LANES):
    b, s, _ = qk3.shape
    gw = GROUP_WIDTH
    n_sel = min(TOPK_MAX, s // 4)
    return pl.pallas_call(
        functools.partial(_dsa_kernel, tq=tq, n_sel=n_sel),
        grid=(b, s // tq),
        in_specs=[pl.BlockSpec((None, tq, gw), lambda bi, i: (bi, i, 0)),
                  pl.BlockSpec((None, s, gw), lambda bi, i: (bi, 0, 1)),
                  pl.BlockSpec((None, gw, s), lambda bi, i: (bi, 0, 0)),
                  pl.BlockSpec((None, tq, IDX_HEADS * IDX_DIM), lambda bi, i: (bi, i, 0)),
                  pl.BlockSpec((None, s, IDX_DIM), lambda bi, i: (bi, 0, 0)),
                  pl.BlockSpec((None, IDX_HEADS, tq), lambda bi, i: (bi, 1, i)),
                  pl.BlockSpec(t5_near.shape, lambda bi, i: (0, 0, 0, 0))],
        out_specs=pl.BlockSpec((None, tq, gw), lambda bi, i: (bi, i, 0)),
        out_shape=jax.ShapeDtypeStruct((b, s, gw), BF16),
        scratch_shapes=[pltpu.VMEM((s, tq), F32)],
        compiler_params=_params("parallel", "parallel"),
        name="dsa_attn",
    )(qk3, qk3, v_t, iq3, ik3, small_t, t5_near)


def _toeplitz(w, rows, cols):
    length = rows + cols - 1
    wp = jnp.pad(w, ((0, 0), (0, 1)))
    flat = jnp.tile(wp, (1, rows))[:, :rows * length]
    return flat.reshape(w.shape[0], rows, length)[:, :, rows - 1:rows - 1 + cols]


def _t5_bucket(rel):
    half = T5_BUCKETS // 2
    max_exact = half // 2
    n = jnp.abs(rel)
    large = max_exact + (jnp.log(jnp.maximum(n, 1).astype(F32) / max_exact)
                         / math.log(T5_MAX_DIST / max_exact) * (half - max_exact)).astype(jnp.int32)
    large = jnp.minimum(large, half - 1)
    return jnp.where(rel > 0, half, 0) + jnp.where(n < max_exact, n, large)


def _t5_near(t5_rel_bias, tq=LANES):
    assert tq >= T5_MAX_DIST
    far = t5_rel_bias[T5_BUCKETS // 2 - 1]
    tables = []
    for shift in range(2):
        rel = (1 + shift) * tq - 1 - np.arange(4 * tq - 1, dtype=np.int32)
        vals = t5_rel_bias[_t5_bucket(jnp.asarray(rel))]
        tables.append(_toeplitz((vals - far[None, :]).T.astype(F32), 3 * tq, tq))
    return jnp.stack(tables, axis=1)


def _mla_prep_kernel(misc_ref, qn_ref, kvn_ref, wqa_ref, wqb_ref, wk_ref, wv_ref, ea_ref, eb_ref,
                     cos_ref, sin_ref, q_ref, k_ref, v_ref):
    misc = misc_ref[...]
    cq = misc[:, CQ_OFF:CQ_OFF + 256]
    ckv = misc[:, CKV_OFF:CKV_OFF + KV_LORA]
    small = misc[:, SMALL_OFF:SMALL_OFF + 128]
    cqn = cq * lax.rsqrt(jnp.sum(cq * cq, axis=1, keepdims=True) / Q_LORA + RMS_EPS) * qn_ref[...]
    ckvn = ckv * lax.rsqrt(jnp.mean(ckv * ckv, axis=1, keepdims=True) + RMS_EPS) * kvn_ref[...]
    cqb, ckvb = cqn.astype(BF16), ckvn.astype(BF16)
    cos, sin = cos_ref[...], sin_ref[...]
    q = (_dot(cqb, wqa_ref[...]) * cos + _dot(cqb, wqb_ref[...]) * sin) * (MLA_NOPE + MLA_ROPE) ** -0.5
    s_hi = small.astype(BF16)
    s_lo = (small - s_hi.astype(F32)).astype(BF16)
    kr_a = _dot(s_hi, ea_ref[...]) + _dot(s_lo, ea_ref[...])
    kr_b = _dot(s_hi, eb_ref[...]) + _dot(s_lo, eb_ref[...])
    k = (_dot(ckvb, wk_ref[...]) + kr_a) * cos + kr_b * sin
    q_ref[...] = q.astype(BF16)
    k_ref[...] = k.astype(BF16)
    v_ref[...] = _dot(ckvb, wv_ref[...]).astype(BF16)


def _mla_prep(misc, qn, kvn, wqa, wqb, wk, wv, ea, eb, cos_t, sin_t, s, tm=512):
    n = misc.shape[0]
    pos_blocks = s // tm
    full = lambda a: pl.BlockSpec(a.shape, lambda i: (0,) * a.ndim)
    hw = GROUP_HEADS * LANES
    return pl.pallas_call(
        _mla_prep_kernel,
        grid=(n // tm,),
        in_specs=[pl.BlockSpec((tm, MISC_W), lambda i: (i, 0)),
                  full(qn), full(kvn), full(wqa), full(wqb), full(wk), full(wv), full(ea), full(eb),
                  pl.BlockSpec((tm, hw), lambda i: (i % pos_blocks, 0)),
                  pl.BlockSpec((tm, hw), lambda i: (i % pos_blocks, 0))],
        out_specs=[pl.BlockSpec((tm, hw), lambda i: (i, 0)),
                   pl.BlockSpec((tm, hw), lambda i: (i, 0)),
                   pl.BlockSpec((tm, GROUP_WIDTH), lambda i: (i, 0))],
        out_shape=[jax.ShapeDtypeStruct((n, hw), BF16),
                   jax.ShapeDtypeStruct((n, hw), BF16),
                   jax.ShapeDtypeStruct((n, GROUP_WIDTH), BF16)],
        compiler_params=_params("parallel"),
        name="mla_prep",
    )(misc, qn, kvn, wqa, wqb, wk, wv, ea, eb, cos_t, sin_t)


def _mla_kernel(q_ref, k_ref, v_ref, o_ref, *, tq):
    def tile(w):
        q, k, v = q_ref[...], k_ref[:w, :], v_ref[:w, :]
        row = lax.broadcasted_iota(jnp.int32, (tq, tq), 0)
        col = lax.broadcasted_iota(jnp.int32, (tq, tq), 1)
        ok = col < _chunk_end(row)
        outs = []
        for h in range(GROUP_HEADS):
            hl = slice(h * LANES, (h + 1) * LANES)
            lg = _mask_last_block(_dot_nt(q[:, hl], k[:, hl]), ok)
            outs.append(_softmax_pv(lg, v[:, h * HEAD_DIM:(h + 1) * HEAD_DIM]))
        o_ref[...] = jnp.concatenate(outs, axis=1).astype(o_ref.dtype)

    _per_tile(tile, k_ref.shape[0] // tq, tq)


def _mla_attn(q3, k3, v3, tq=256):
    b, s, hw = q3.shape
    gw = GROUP_WIDTH
    return pl.pallas_call(
        functools.partial(_mla_kernel, tq=tq),
        grid=(b, s // tq),
        in_specs=[pl.BlockSpec((None, tq, hw), lambda bi, i: (bi, i, 0)),
                  pl.BlockSpec((None, s, hw), lambda bi, i: (bi, 0, 0)),
                  pl.BlockSpec((None, s, gw), lambda bi, i: (bi, 0, 0))],
        out_specs=pl.BlockSpec((None, tq, gw), lambda bi, i: (bi, i, 0)),
        out_shape=jax.ShapeDtypeStruct((b, s, gw), BF16),
        compiler_params=_params("parallel", "parallel"),
        name="mla_attn",
    )(q3, k3, v3)


def _mla_weights(w_uq, w_ukv):
    half = MLA_ROPE // 2
    hw = GROUP_HEADS * LANES
    wqa = jnp.zeros((256, hw), F32)
    wqb = jnp.zeros((256, hw), F32)
    wk = jnp.zeros((KV_LORA, hw), F32)
    wv = jnp.zeros((KV_LORA, GROUP_WIDTH), F32)
    for h in range(GROUP_HEADS):
        qh = w_uq[:, h * (MLA_NOPE + MLA_ROPE):(h + 1) * (MLA_NOPE + MLA_ROPE)]
        nope, r1, r2 = qh[:, :MLA_NOPE], qh[:, MLA_NOPE:MLA_NOPE + half], qh[:, MLA_NOPE + half:]
        o = h * LANES
        wqa = wqa.at[:Q_LORA, o:o + MLA_NOPE].set(nope)
        wqa = wqa.at[:Q_LORA, o + MLA_NOPE:o + MLA_NOPE + half].set(r1)
        wqa = wqa.at[:Q_LORA, o + MLA_NOPE + half:o + MLA_NOPE + 2 * half].set(r2)
        wqb = wqb.at[:Q_LORA, o + MLA_NOPE:o + MLA_NOPE + half].set(r2)
        wqb = wqb.at[:Q_LORA, o + MLA_NOPE + half:o + MLA_NOPE + 2 * half].set(r1)
        kvh = w_ukv[:, h * 2 * HEAD_DIM:(h + 1) * 2 * HEAD_DIM]
        wk = wk.at[:, o:o + MLA_NOPE].set(kvh[:, :MLA_NOPE])
        wv = wv.at[:, h * HEAD_DIM:(h + 1) * HEAD_DIM].set(kvh[:, MLA_NOPE:])
    return wqa.astype(BF16), wqb.astype(BF16), wk.astype(BF16), wv.astype(BF16)


def _mla_tables(s):
    half = MLA_ROPE // 2
    hw = GROUP_HEADS * LANES
    inv_freq = ROPE_THETA ** (-jnp.arange(half, dtype=F32) / half)
    ang = jnp.arange(s).astype(F32)[:, None] * inv_freq[None, :]
    c, sn = jnp.cos(ang), jnp.sin(ang)
    head_cos = jnp.concatenate([jnp.ones((s, MLA_NOPE), F32), c, c, jnp.zeros((s, LANES - MLA_NOPE - 2 * half), F32)], axis=1)
    head_sin = jnp.concatenate([jnp.zeros((s, MLA_NOPE), F32), -sn, sn, jnp.zeros((s, LANES - MLA_NOPE - 2 * half), F32)], axis=1)
    cos_t = jnp.concatenate([head_cos] * GROUP_HEADS, axis=1)
    sin_t = jnp.concatenate([head_sin] * GROUP_HEADS, axis=1)
    ea = np.zeros((128, hw), np.float32)
    eb = np.zeros((128, hw), np.float32)
    kr = KR_OFF - SMALL_OFF
    for h in range(GROUP_HEADS):
        o = h * LANES + MLA_NOPE
        for j in range(half):
            ea[kr + j, o + j] = 1.0
            ea[kr + half + j, o + half + j] = 1.0
            eb[kr + half + j, o + j] = 1.0
            eb[kr + j, o + half + j] = 1.0
    return cos_t, sin_t, jnp.asarray(ea, BF16), jnp.asarray(eb, BF16)


def _band_kernel(q_ref, k_ref, v_ref, bias_ref, o_ref, *, tq):
    left = LEFT_CHUNKS * CHUNK

    def tile(end):
        first = max(0, end - tq - left)
        w = end - first
        q = q_ref[...] * HEAD_DIM ** -0.5
        k, v = k_ref[first:end, :], v_ref[first:end, :]
        outs = []
        for h in range(GROUP_HEADS):
            sl = slice(h * HEAD_DIM, (h + 1) * HEAD_DIM)
            lg = _dot_nt(q[:, sl], k[:, sl]) + bias_ref[h, :, left + tq - w:]
            outs.append(_softmax_pv(lg, v[:, sl]))
        o_ref[...] = jnp.concatenate(outs, axis=1).astype(o_ref.dtype)

    _per_tile(tile, k_ref.shape[0] // tq, tq)


def _band_attn(qkv3, bias, tq=LANES):
    b, s, _ = qkv3.shape
    gw = GROUP_WIDTH
    return pl.pallas_call(
        functools.partial(_band_kernel, tq=tq),
        grid=(b, s // tq),
        in_specs=[pl.BlockSpec((None, tq, gw), lambda bi, i: (bi, i, 0)),
                  pl.BlockSpec((None, s, gw), lambda bi, i: (bi, 0, 1)),
                  pl.BlockSpec((None, s, gw), lambda bi, i: (bi, 0, 2)),
                  pl.BlockSpec(bias.shape, lambda bi, i: (0, 0, 0))],
        out_specs=pl.BlockSpec((None, tq, gw), lambda bi, i: (bi, i, 0)),
        out_shape=jax.ShapeDtypeStruct((b, s, gw), BF16),
        compiler_params=_params("parallel", "parallel"),
        name="band_attn",
    )(qkv3, qkv3, qkv3, bias)


def _band_bias(rel_table, tq=LANES):
    pad = LEFT_CHUNKS * CHUNK
    win = pad + tq
    i = np.arange(tq)[:, None]
    r = np.arange(win)[None, :]
    kc = r // CHUNK - LEFT_CHUNKS
    qc = i // CHUNK
    in_band = (kc <= qc) & (kc >= qc - LEFT_CHUNKS)
    diag = pad + tq - 1 - np.arange(tq + win - 1)
    bias = _toeplitz(rel_table[:, np.clip(diag, -MAX_REL, MAX_REL) + MAX_REL].astype(F32), tq, win)
    return jnp.where(jnp.asarray(in_band)[None], bias, NEG)


def _outproj_kernel(a_ref, b_ref, c_ref, d_ref, w_ref, x_ref, g_ref, bt_ref, wr_ref, rb_ref,
                    y_ref, gate_ref, *, alpha):
    acc = alpha * x_ref[...]
    for j, ref in enumerate((a_ref, b_ref, c_ref, d_ref)):
        acc = acc + _dot(ref[...], w_ref[j * GROUP_WIDTH:(j + 1) * GROUP_WIDTH, :])
    y = _layer_norm(acc, g_ref[...], bt_ref[...])
    y_ref[...] = y

    y_hi = y.astype(BF16)
    y_lo = (y - y_hi.astype(F32)).astype(BF16)
    logits = _dot_nt(wr_ref[0], y_hi) + (_dot_nt(wr_ref[0], y_lo) + _dot_nt(wr_ref[1], y_hi))
    score = 1.0 / (1.0 + jnp.exp(-logits))
    pick = score + rb_ref[...]
    row = lax.broadcasted_iota(jnp.int32, score.shape, 0).astype(F32)
    raw = jnp.zeros_like(score)
    for _ in range(TOP_K):
        best = jnp.max(pick, axis=0, keepdims=True)
        first = jnp.min(jnp.where(pick == best, row, float(LANES)), axis=0, keepdims=True)
        hit = row == first
        raw = jnp.where(hit, score, raw)
        pick = jnp.where(hit, -jnp.inf, pick)
    gates = (raw / jnp.sum(raw, axis=0, keepdims=True) * ROUTED_SCALE).T
    for grp in range(gate_ref.shape[0]):
        gate_ref[grp] = gates[:, grp * MOE_GROUP:(grp + 1) * MOE_GROUP]


def _outproj_ln_router(mix, w_out, x2, g, bt, w_router2, router_bias, alpha, tm=512):
    n = x2.shape[0]
    n_groups = N_EXPERTS // MOE_GROUP
    full = lambda a: pl.BlockSpec(a.shape, lambda i: (0,) * a.ndim)
    return pl.pallas_call(
        functools.partial(_outproj_kernel, alpha=alpha),
        grid=(n // tm,),
        in_specs=[pl.BlockSpec((tm, GROUP_WIDTH), lambda i: (i, 0))] * 4
                 + [full(w_out), pl.BlockSpec((tm, D_MODEL), lambda i: (i, 0)),
                    full(g), full(bt), full(w_router2), full(router_bias)],
        out_specs=[pl.BlockSpec((tm, D_MODEL), lambda i: (i, 0)),
                   pl.BlockSpec((n_groups, tm, MOE_GROUP), lambda i: (0, i, 0))],
        out_shape=[jax.ShapeDtypeStruct((n, D_MODEL), F32),
                   jax.ShapeDtypeStruct((n_groups, n, MOE_GROUP), F32)],
        compiler_params=_params("parallel"),
        name="outproj_ln_router",
    )(*mix, w_out, x2, g, bt, w_router2, router_bias)


def _moe_kernel(x_ref, gate_ref, wgu_ref, wd_ref, wsgu_ref, wsd_ref, g_ref, bt_ref, o_ref,
                acc_ref, xb_ref, *, alpha, group):
    e = pl.program_id(1)

    @pl.when(e == 0)
    def _():
        xb = x_ref[...].astype(BF16)
        xb_ref[...] = xb
        hs = _dot(xb, wsgu_ref[...])
        a = _silu(hs[:, :D_SHARED]) * hs[:, D_SHARED:]
        acc_ref[...] = _dot(a.astype(BF16), wsd_ref[...])

    xb = xb_ref[...]
    gates = gate_ref[...]
    acts = []
    for j in range(group):
        h = _dot(xb, wgu_ref[j])
        a = _silu(h[:, :D_EXPERT]) * h[:, D_EXPERT:] * gates[:, j:j + 1]
        acts.append(a.astype(BF16))
    w_down = wd_ref[...].reshape(group * D_EXPERT, D_MODEL)
    acc_ref[...] += _dot(jnp.concatenate(acts, axis=1), w_down)

    @pl.when(e == pl.num_programs(1) - 1)
    def _():
        y = alpha * x_ref[...] + acc_ref[...]
        o_ref[...] = _layer_norm(y, g_ref[...], bt_ref[...])


def _moe_ln(x2, gates_g, w_gu, w_down, w_sh_gu, w_sh_down, g, bt, layer, alpha, tm=1024):
    n = x2.shape[0]
    n_groups, _, group = gates_g.shape
    full = lambda a: pl.BlockSpec(a.shape, lambda i, e: (0,) * a.ndim)
    return pl.pallas_call(
        functools.partial(_moe_kernel, alpha=alpha, group=group),
        grid=(n // tm, n_groups),
        in_specs=[pl.BlockSpec((tm, D_MODEL), lambda i, e: (i, 0)),
                  pl.BlockSpec((None, tm, group), lambda i, e: (e, i, 0)),
                  pl.BlockSpec((None, group, D_MODEL, 2 * D_EXPERT), lambda i, e: (layer, e, 0, 0)),
                  pl.BlockSpec((None, group, D_EXPERT, D_MODEL), lambda i, e: (layer, e, 0, 0)),
                  full(w_sh_gu), full(w_sh_down), full(g), full(bt)],
        out_specs=pl.BlockSpec((tm, D_MODEL), lambda i, e: (i, 0)),
        out_shape=jax.ShapeDtypeStruct((n, D_MODEL), F32),
        scratch_shapes=[pltpu.VMEM((tm, D_MODEL), F32), pltpu.VMEM((tm, D_MODEL), BF16)],
        compiler_params=_params("parallel", "arbitrary"),
        name="moe_ln",
    )(x2, gates_g, w_gu, w_down, w_sh_gu, w_sh_down, g, bt)


def _inproj_weights(w_in):
    sizes = (3 * GROUP_WIDTH, GROUP_HEADS, 3 * GROUP_WIDTH, IDX_HEADS * IDX_DIM, IDX_DIM, IDX_HEADS,
             Q_LORA, KV_LORA, MLA_ROPE, 3 * GROUP_WIDTH)
    splits = tuple(int(v) for v in np.cumsum(sizes)[:-1])
    (fox, fox_f, dsa, idx_q, idx_k, idx_w, c_q, c_kv, k_r, band) = jnp.split(w_in, splits, axis=-1)
    lead = w_in.shape[:-1]
    misc = jnp.zeros(lead + (MISC_W,), w_in.dtype)
    for off, part in ((CQ_OFF, c_q), (CKV_OFF, c_kv), (FF_OFF, fox_f), (IW_OFF, idx_w), (IK_OFF, idx_k), (KR_OFF, k_r)):
        misc = misc.at[..., off:off + part.shape[-1]].set(part)
    dsa_qk, dsa_v = dsa[..., :2 * GROUP_WIDTH], dsa[..., 2 * GROUP_WIDTH:]
    w_re = jnp.concatenate([fox, dsa_qk, band, idx_q, misc], axis=-1).astype(BF16)
    w_t = jnp.concatenate([dsa_v, misc[..., SMALL_OFF:SMALL_OFF + SMALL_T_ROWS]], axis=-1)
    return w_re, jnp.swapaxes(w_t, -1, -2).astype(BF16)


def _pad_lanes(a, width, value=0.0):
    return jnp.pad(a, [(0, 0)] * (a.ndim - 1) + [(0, width - a.shape[-1])], constant_values=value)


def kernel(x, w_in, b_forget, mla_q_norm, mla_kv_norm, w_mla_uq, w_mla_ukv, t5_rel_bias, chunk_rel_bias,
           w_out, ln1_g, ln1_b, w_router, router_bias, w_exp_gu, w_exp_down, w_sh_gu, w_sh_down, ln2_g, ln2_b):
    b, s, d = x.shape
    depth = w_in.shape[0]
    n = b * s
    alpha = (2 * depth) ** 0.25

    w_in_re, w_in_t = _inproj_weights(w_in)
    t5_near = _t5_near(t5_rel_bias)
    cos_t, sin_t, ea, eb = _mla_tables(s)
    w_out_b = w_out.astype(BF16)
    w_gu_b, w_down_b = w_exp_gu.astype(BF16), w_exp_down.astype(BF16)
    w_sh_gu_b, w_sh_down_b = w_sh_gu.astype(BF16), w_sh_down.astype(BF16)
    w_router_t = jnp.swapaxes(_pad_lanes(w_router, LANES), -1, -2)
    w_router_hi = w_router_t.astype(BF16)
    w_router_lo = (w_router_t - w_router_hi.astype(F32)).astype(BF16)
    w_router2 = jnp.stack([w_router_hi, w_router_lo], axis=1)
    router_bias_p = _pad_lanes(router_bias, LANES, -jnp.inf)[..., None]

    x2 = x.reshape(n, d)
    for l in range(depth):
        fox, dsa_qk, band, iq, misc, dsa_vt, small_t = _inproj(x2, w_in_re[l], w_in_t[l], s)
        misc3 = misc.reshape(b, s, MISC_W)

        cum_row = _fox_cum(small_t, _pad_lanes(b_forget[l][None, :], 8).T)
        o_fox = _fox_attn(fox.reshape(b, s, -1), cum_row)

        ik3 = misc3[:, :, IK_OFF:IK_OFF + IDX_DIM].astype(BF16)
        o_dsa = _dsa_attn(dsa_qk.reshape(b, s, -1), dsa_vt, iq.reshape(b, s, -1), ik3, small_t, t5_near)

        wqa, wqb, wk, wv = _mla_weights(w_mla_uq[l], w_mla_ukv[l])
        q_m, k_m, v_m = _mla_prep(misc, _pad_lanes(mla_q_norm[l][None, :], 256), mla_kv_norm[l][None, :],
                                  wqa, wqb, wk, wv, ea, eb, cos_t, sin_t, s)
        o_mla = _mla_attn(q_m.reshape(b, s, -1), k_m.reshape(b, s, -1), v_m.reshape(b, s, -1))

        o_band = _band_attn(band.reshape(b, s, -1), _band_bias(chunk_rel_bias[l], BAND_TQ), BAND_TQ)

        mix = [o.reshape(n, GROUP_WIDTH) for o in (o_fox, o_dsa, o_mla, o_band)]
        x2, gates_g = _outproj_ln_router(mix, w_out_b[l], x2, ln1_g[l][None, :], ln1_b[l][None, :],
                                         w_router2[l], router_bias_p[l], alpha)
        x2 = _moe_ln(x2, gates_g, w_gu_b, w_down_b, w_sh_gu_b[l], w_sh_down_b[l],
                     ln2_g[l][None, :], ln2_b[l][None, :], l, alpha)
    return x2.reshape(b, s, d)
```

```python
import functools
import math

import numpy as np
import jax
import jax.numpy as jnp
from jax import lax
from jax.experimental import pallas as pl
from jax.experimental.pallas import tpu as pltpu

D_MODEL = 1024
CHUNK = 64
HEAD_DIM = 64
GROUP_HEADS = 4
GROUP_WIDTH = GROUP_HEADS * HEAD_DIM
IDX_HEADS = 8
IDX_DIM = 32
TOPK_MAX = 256
T5_BUCKETS = 32
T5_MAX_DIST = 128
Q_LORA = 192
KV_LORA = 128
MLA_NOPE = 64
MLA_ROPE = 32
ROPE_THETA = 10000.0
LEFT_CHUNKS = 8
MAX_REL = 256
N_EXPERTS = 64
TOP_K = 8
D_EXPERT = 256
D_SHARED = 256
ROUTED_SCALE = 2.5
MOE_GROUP = 4
BAND_TQ = 256
DSA_TQ = 256
DSA_BUCKET = 256
DSA_NEAR = DSA_BUCKET + T5_MAX_DIST
LN_EPS = 1e-5
RMS_EPS = 1e-6

LANES = 128
NEG = -1e30
BISECT_STEPS = 16
VMEM_LIMIT = 56 * 1024 * 1024

MISC_W = 512
CQ_OFF, CKV_OFF, SMALL_OFF = 0, 256, 384
FF_OFF, IW_OFF, IK_OFF, KR_OFF = 384, 392, 416, 448
SMALL_T_ROWS = 16

BF16 = jnp.bfloat16
F32 = jnp.float32


def _dot(a, b):
    return jnp.dot(a, b, preferred_element_type=F32)


def _dot_nt(a, b):
    return lax.dot_general(a, b, (((1,), (1,)), ((), ())), preferred_element_type=F32)


def _params(*sem):
    return pltpu.CompilerParams(dimension_semantics=sem, vmem_limit_bytes=VMEM_LIMIT)


def _layer_norm(y, g, b):
    mu = jnp.mean(y, axis=-1, keepdims=True)
    yc = y - mu
    var = jnp.mean(yc * yc, axis=-1, keepdims=True)
    return yc * lax.rsqrt(var + LN_EPS) * g + b


def _chunk_end(pos):
    return (jnp.right_shift(pos, CHUNK.bit_length() - 1) + 1) * CHUNK


def _silu(x):
    return x / (1.0 + jnp.exp(-x))


def _softmax_pv(s, v):
    m = jnp.max(s, axis=1, keepdims=True)
    p = jnp.exp(s - m)
    l = jnp.sum(p, axis=1, keepdims=True)
    return _dot(p.astype(BF16), v) / l


def _inproj_kernel(x_ref, w_ref, wt_ref, fox_ref, dsa_ref, band_ref, iq_ref, misc_ref, vt_ref, small_t_ref):
    xb = x_ref[...].astype(BF16)
    off = 0
    for ref in (fox_ref, dsa_ref, band_ref, iq_ref, misc_ref):
        w = ref.shape[1]
        ref[...] = _dot(xb, w_ref[:, off:off + w]).astype(ref.dtype)
        off += w
    t = _dot_nt(wt_ref[...], xb)
    vt_ref[...] = t[:GROUP_WIDTH, :].astype(vt_ref.dtype)
    small_t_ref[...] = t[GROUP_WIDTH:, :]


def _inproj(x2, w_re, w_t, s, tm=512):
    n = x2.shape[0]
    per_seq = s // tm
    widths = (3 * GROUP_WIDTH, 2 * GROUP_WIDTH, 3 * GROUP_WIDTH, IDX_HEADS * IDX_DIM, MISC_W)
    dtypes = (BF16, BF16, BF16, BF16, F32)
    t_rows = (GROUP_WIDTH, SMALL_T_ROWS)
    t_dtypes = (BF16, F32)
    return pl.pallas_call(
        _inproj_kernel,
        grid=(n // tm,),
        in_specs=[pl.BlockSpec((tm, D_MODEL), lambda i: (i, 0)),
                  pl.BlockSpec(w_re.shape, lambda i: (0, 0)),
                  pl.BlockSpec(w_t.shape, lambda i: (0, 0))],
        out_specs=[pl.BlockSpec((tm, w), lambda i: (i, 0)) for w in widths]
                  + [pl.BlockSpec((None, r, tm), lambda i: (i // per_seq, 0, i % per_seq)) for r in t_rows],
        out_shape=[jax.ShapeDtypeStruct((n, w), d) for w, d in zip(widths, dtypes)]
                  + [jax.ShapeDtypeStruct((n // s, r, s), d) for r, d in zip(t_rows, t_dtypes)],
        compiler_params=_params("parallel"),
        name="inproj",
    )(x2, w_re, w_t)


def _cum_kernel(f_ref, b_ref, o_ref):
    x = f_ref[...] + b_ref[...]
    acc = jnp.minimum(x, 0.0) - jnp.log(1.0 + jnp.exp(-jnp.abs(x)))
    s = x.shape[1]
    lane = lax.broadcasted_iota(jnp.int32, x.shape, 1)
    d = 1
    while d < s:
        acc = acc + jnp.where(lane >= d, pltpu.roll(acc, d, 1), 0.0)
        d *= 2
    o_ref[...] = acc


def _fox_cum(f_t, b_col):
    b, _, s = f_t.shape
    r = b_col.shape[0]
    return pl.pallas_call(
        _cum_kernel,
        grid=(b,),
        in_specs=[pl.BlockSpec((None, r, s), lambda i: (i, 0, 0)),
                  pl.BlockSpec((r, 1), lambda i: (0, 0))],
        out_specs=pl.BlockSpec((None, r, s), lambda i: (i, 0, 0)),
        out_shape=jax.ShapeDtypeStruct((b, r, s), F32),
        compiler_params=_params("parallel"),
        name="fox_cum",
    )(f_t, b_col)


def _mask_last_block(lg, ok):
    tq = ok.shape[1]
    w = lg.shape[1]
    last = jnp.where(ok, lg[:, w - tq:], NEG)
    return last if w == tq else jnp.concatenate([lg[:, :w - tq], last], axis=1)


def _per_tile(body, n_tiles, tq):
    i = pl.program_id(1)
    for j in range(n_tiles):
        pl.when(i == j)(functools.partial(body, (j + 1) * tq))


def _fox_kernel(q_ref, k_ref, v_ref, ck_ref, o_ref, *, tq):
    def tile(w):
        q = q_ref[...] * HEAD_DIM ** -0.5
        k, v, ck = k_ref[:w, :], v_ref[:w, :], ck_ref[:, :w]
        ok = (lax.broadcasted_iota(jnp.int32, (tq, tq), 1) <= lax.broadcasted_iota(jnp.int32, (tq, tq), 0))
        outs = []
        for h in range(GROUP_HEADS):
            sl = slice(h * HEAD_DIM, (h + 1) * HEAD_DIM)
            lg = _dot_nt(q[:, sl], k[:, sl]) - ck[h:h + 1, :]
            outs.append(_softmax_pv(_mask_last_block(lg, ok), v[:, sl]))
        o_ref[...] = jnp.concatenate(outs, axis=1).astype(o_ref.dtype)

    _per_tile(tile, k_ref.shape[0] // tq, tq)


def _fox_attn(qkv3, cum_row, tq=256):
    b, s, _ = qkv3.shape
    gw = GROUP_WIDTH
    return pl.pallas_call(
        functools.partial(_fox_kernel, tq=tq),
        grid=(b, s // tq),
        in_specs=[pl.BlockSpec((None, tq, gw), lambda bi, i: (bi, i, 0)),
                  pl.BlockSpec((None, s, gw), lambda bi, i: (bi, 0, 1)),
                  pl.BlockSpec((None, s, gw), lambda bi, i: (bi, 0, 2)),
                  pl.BlockSpec((None, 8, s), lambda bi, i: (bi, 0, 0))],
        out_specs=pl.BlockSpec((None, tq, gw), lambda bi, i: (bi, i, 0)),
        out_shape=jax.ShapeDtypeStruct((b, s, gw), BF16),
        compiler_params=_params("parallel", "parallel"),
        name="fox_attn",
    )(qkv3, qkv3, qkv3, cum_row)


def _over_keys(reduce, x, chains=4):
    w, tq = x.shape
    sub = 8
    part = reduce(x.reshape(chains, w // (sub * chains), sub, tq), axis=1)
    return reduce(reduce(part, axis=0), axis=0, keepdims=True)


def _kth_largest(sc_ref, w, lo, hi, searching, kf):
    def count_ge(t):
        return _over_keys(jnp.sum, jnp.where(sc_ref[:w, :] >= t, 1.0, 0.0))

    def halve(_, bracket):
        lo, hi = bracket
        mid = lo + (hi - lo) * 0.5
        more = count_ge(mid) >= kf
        return jnp.where(more, mid, lo), jnp.where(more, hi, mid)

    _, hi = lax.fori_loop(0, BISECT_STEPS, halve, (lo, hi))

    def unresolved(state):
        hi, thr, n_ge, open_, it = state
        return jnp.logical_and(jnp.max(open_) > 0.0, it <= w)

    def peel(state):
        hi, thr, n_ge, open_, it = state
        sc = sc_ref[:w, :]
        cand = _over_keys(jnp.max, jnp.where(sc < hi, sc, -jnp.inf))
        c = count_ge(cand)
        found = (open_ > 0.0) & (c >= kf)
        thr = jnp.where(found, cand, thr)
        n_ge = jnp.where(found, c, n_ge)
        open_ = jnp.where(found, 0.0, open_)
        return jnp.where(open_ > 0.0, cand, hi), thr, n_ge, open_, it + 1

    state = (hi, jnp.full_like(lo, jnp.finfo(F32).min), jnp.full_like(lo, kf),
             jnp.where(searching, 1.0, 0.0), jnp.int32(0))
    _, thr, n_ge, _, _ = lax.while_loop(unresolved, peel, state)
    return thr, n_ge


def _dsa_tile(w, q_ref, k_ref, vt_ref, iq_ref, ik_ref, iwt_ref, t5_ref, o_ref, sc_ref, *, tq, n_sel):
    m = pl.program_id(1)
    limit = _chunk_end(m * tq + lax.broadcasted_iota(jnp.int32, (1, tq), 1))
    valid = lax.broadcasted_iota(jnp.int32, (w, tq), 0) < limit

    iq = iq_ref[...]
    ik = ik_ref[:w, IK_OFF - SMALL_OFF:IK_OFF - SMALL_OFF + IDX_DIM].astype(BF16)
    iw = iwt_ref[...] * (IDX_HEADS ** -0.5 * IDX_DIM ** -0.5)
    score = jnp.zeros((w, tq), F32)
    for h in range(IDX_HEADS):
        d = _dot_nt(ik, iq[:, h * IDX_DIM:(h + 1) * IDX_DIM])
        score = score + jnp.maximum(d, 0.0) * iw[h:h + 1, :]
    score = jnp.where(valid, score, -jnp.inf)
    sc_ref[:w, :] = score

    searching = limit > n_sel
    top = _over_keys(jnp.max, score)
    lo0 = jnp.where(searching, _over_keys(jnp.min, jnp.where(valid, score, jnp.inf)), 0.0)
    hi0 = jnp.where(searching, top + jnp.abs(top) * 2.0 ** -20 + 1e-37, 0.0)
    kf = float(n_sel)
    thr, n_ge = _kth_largest(sc_ref, w, lo0, hi0, searching, kf)

    tie_split = jnp.max(jnp.where(searching & (n_ge > kf), 1.0, 0.0)) > 0.0

    @pl.when(tie_split)
    def _drop_surplus_ties():
        score = sc_ref[:w, :]
        room = kf - _over_keys(jnp.sum, jnp.where(score > thr, 1.0, 0.0))
        blk = DSA_BUCKET
        tri =(lax.broadcasted_iota(jnp.int32, (blk, blk), 1)
               <= lax.broadcasted_iota(jnp.int32, (blk, blk), 0)).astype(BF16)
        before = jnp.zeros((1, tq), F32)
        for c in range(w // blk):
            sb = score[c * blk:(c + 1) * blk, :]
            eb = jnp.where(sb == thr, 1.0, 0.0)
            inc = _dot(tri, eb.astype(BF16)) + before
            surplus = (eb > 0.0) & (inc - eb >= room)
            sc_ref[c * blk:(c + 1) * blk, :] = jnp.where(surplus, -jnp.inf, sb)
            before = before + jnp.sum(eb, axis=0, keepdims=True)

    mask = sc_ref[:w, :] >= thr

    near = DSA_NEAR
    shift = (w - (m + 1) * tq) // tq
    qs = q_ref[...] * HEAD_DIM ** -0.5
    k = k_ref[:w, :]
    outs = []
    for h in range(GROUP_HEADS):
        sl = slice(h * HEAD_DIM, (h + 1) * HEAD_DIM)
        lg = _dot_nt(k[:, sl], qs[:, sl])
        tab = t5_ref[h, shift]
        if w < near:
            lg = lg + tab[near - w:, :]
        else:
            biased = lg[w - near:, :] + tab
            lg = biased if w == near else jnp.concatenate([lg[:w - near, :], biased], axis=0)
        lg = jnp.where(mask, lg, NEG)
        p = jnp.exp(lg - _over_keys(jnp.max, lg))
        l = _over_keys(jnp.sum, p)
        outs.append(_dot(vt_ref[sl, :w], p.astype(BF16)) / l)
    o_ref[...] = jnp.concatenate(outs, axis=0).T.astype(o_ref.dtype)


def _dsa_kernel(*refs, tq, n_sel):
    m = pl.program_id(1)
    s = refs[-1].shape[0]
    per = DSA_BUCKET // tq
    for j in range(s // DSA_BUCKET):
        pl.when(m // per == j)(functools.partial(_dsa_tile, (j + 1) * DSA_BUCKET, *refs, tq=tq, n_sel=n_sel))


def _dsa_attn(qk3, v_t, iq3, misc3, small_t, t5_near, tq):
    b, s, _ = qk3.shape
    gw = GROUP_WIDTH
    n_sel = min(TOPK_MAX, s // 4)
    return pl.pallas_call(
        functools.partial(_dsa_kernel, tq=tq, n_sel=n_sel),
        grid=(b, s // tq),
        in_specs=[pl.BlockSpec((None, tq, gw), lambda bi, i: (bi, i, 0)),
                  pl.BlockSpec((None, s, gw), lambda bi, i: (bi, 0, 1)),
                  pl.BlockSpec((None, gw, s), lambda bi, i: (bi, 0, 0)),
                  pl.BlockSpec((None, tq, IDX_HEADS * IDX_DIM), lambda bi, i: (bi, i, 0)),
                  pl.BlockSpec((None, s, LANES), lambda bi, i: (bi, 0, SMALL_OFF // LANES)),
                  pl.BlockSpec((None, IDX_HEADS, tq), lambda bi, i: (bi, 1, i)),
                  pl.BlockSpec(t5_near.shape, lambda bi, i: (0, 0, 0, 0))],
        out_specs=pl.BlockSpec((None, tq, gw), lambda bi, i: (bi, i, 0)),
        out_shape=jax.ShapeDtypeStruct((b, s, gw), BF16),
        scratch_shapes=[pltpu.VMEM((s, tq), F32)],
        compiler_params=_params("parallel", "parallel"),
        name="dsa_attn",
    )(qk3, qk3, v_t, iq3, misc3, small_t, t5_near)


def _toeplitz(w, rows, cols):
    length = rows + cols - 1
    wp = jnp.pad(w, ((0, 0), (0, 1)))
    flat = jnp.tile(wp, (1, rows))[:, :rows * length]
    return flat.reshape(w.shape[0], rows, length)[:, :, rows - 1:rows - 1 + cols]


def _t5_bucket(rel):
    half = T5_BUCKETS // 2
    max_exact = half // 2
    n = jnp.abs(rel)
    large = max_exact + (jnp.log(jnp.maximum(n, 1).astype(F32) / max_exact)
                         / math.log(T5_MAX_DIST / max_exact) * (half - max_exact)).astype(jnp.int32)
    large = jnp.minimum(large, half - 1)
    return jnp.where(rel > 0, half, 0) + jnp.where(n < max_exact, n, large)


def _t5_near(t5_rel_bias, tq):
    assert tq >= T5_MAX_DIST and DSA_BUCKET % tq == 0
    near = DSA_NEAR
    far = t5_rel_bias[T5_BUCKETS // 2 - 1]
    tables = []
    for shift in range(DSA_BUCKET // tq):
        rel = (1 + shift) * tq - 1 - np.arange(near + tq - 1, dtype=np.int32)
        vals = jnp.stack([t5_rel_bias[:, h][_t5_bucket(jnp.asarray(rel))] for h in range(GROUP_HEADS)])
        tables.append(_toeplitz((vals - far[:, None]).astype(F32), near, tq))
    return jnp.stack(tables, axis=1)


def _mla_prep_kernel(misc_ref, qn_ref, kvn_ref, wqa_ref, wqb_ref, wk_ref, wv_ref, ea_ref, eb_ref,
                     cos_ref, sin_ref, q_ref, k_ref, v_ref):
    misc = misc_ref[...]
    cq = misc[:, CQ_OFF:CQ_OFF + 256]
    ckv = misc[:, CKV_OFF:CKV_OFF + KV_LORA]
    small = misc[:, SMALL_OFF:SMALL_OFF + 128]
    cqn = cq * lax.rsqrt(jnp.sum(cq * cq, axis=1, keepdims=True) / Q_LORA + RMS_EPS) * qn_ref[...]
    ckvn = ckv * lax.rsqrt(jnp.mean(ckv * ckv, axis=1, keepdims=True) + RMS_EPS) * kvn_ref[...]
    cqb, ckvb = cqn.astype(BF16), ckvn.astype(BF16)
    cos, sin = cos_ref[...], sin_ref[...]
    q = (_dot(cqb, wqa_ref[...]) * cos + _dot(cqb, wqb_ref[...]) * sin) * (MLA_NOPE + MLA_ROPE) ** -0.5
    s_hi = small.astype(BF16)
    s_lo = (small - s_hi.astype(F32)).astype(BF16)
    kr_a = _dot(s_hi, ea_ref[...]) + _dot(s_lo, ea_ref[...])
    kr_b = _dot(s_hi, eb_ref[...]) + _dot(s_lo, eb_ref[...])
    k = (_dot(ckvb, wk_ref[...]) + kr_a) * cos + kr_b * sin
    q_ref[...] = q.astype(BF16)
    k_ref[...] = k.astype(BF16)
    v_ref[...] = _dot(ckvb, wv_ref[...]).astype(BF16)


def _mla_prep(misc, qn, kvn, wqa, wqb, wk, wv, ea, eb, cos_t, sin_t, s, tm=512):
    n = misc.shape[0]
    pos_blocks = s // tm
    full = lambda a: pl.BlockSpec(a.shape, lambda i: (0,) * a.ndim)
    hw = GROUP_HEADS * LANES
    return pl.pallas_call(
        _mla_prep_kernel,
        grid=(n // tm,),
        in_specs=[pl.BlockSpec((tm, MISC_W), lambda i: (i, 0)),
                  full(qn), full(kvn), full(wqa), full(wqb), full(wk), full(wv), full(ea), full(eb),
                  pl.BlockSpec((tm, hw), lambda i: (i % pos_blocks, 0)),
                  pl.BlockSpec((tm, hw), lambda i: (i % pos_blocks, 0))],
        out_specs=[pl.BlockSpec((tm, hw), lambda i: (i, 0)),
                   pl.BlockSpec((tm, hw), lambda i: (i, 0)),
                   pl.BlockSpec((tm, GROUP_WIDTH), lambda i: (i, 0))],
        out_shape=[jax.ShapeDtypeStruct((n, hw), BF16),
                   jax.ShapeDtypeStruct((n, hw), BF16),
                   jax.ShapeDtypeStruct((n, GROUP_WIDTH), BF16)],
        compiler_params=_params("parallel"),
        name="mla_prep",
    )(misc, qn, kvn, wqa, wqb, wk, wv, ea, eb, cos_t, sin_t)


def _mla_kernel(q_ref, k_ref, v_ref, o_ref, *, tq):
    def tile(w):
        q, k, v = q_ref[...], k_ref[:w, :], v_ref[:w, :]
        row = lax.broadcasted_iota(jnp.int32, (tq, tq), 0)
        col = lax.broadcasted_iota(jnp.int32, (tq, tq), 1)
        ok = col < _chunk_end(row)
        outs = []
        for h in range(GROUP_HEADS):
            hl = slice(h * LANES, (h + 1) * LANES)
            lg = _mask_last_block(_dot_nt(q[:, hl], k[:, hl]), ok)
            outs.append(_softmax_pv(lg, v[:, h * HEAD_DIM:(h + 1) * HEAD_DIM]))
        o_ref[...] = jnp.concatenate(outs, axis=1).astype(o_ref.dtype)

    _per_tile(tile, k_ref.shape[0] // tq, tq)


def _mla_attn(q3, k3, v3, tq=256):
    b, s, hw = q3.shape
    gw = GROUP_WIDTH
    return pl.pallas_call(
        functools.partial(_mla_kernel, tq=tq),
        grid=(b, s // tq),
        in_specs=[pl.BlockSpec((None, tq, hw), lambda bi, i: (bi, i, 0)),
                  pl.BlockSpec((None, s, hw), lambda bi, i: (bi, 0, 0)),
                  pl.BlockSpec((None, s, gw), lambda bi, i: (bi, 0, 0))],
        out_specs=pl.BlockSpec((None, tq, gw), lambda bi, i: (bi, i, 0)),
        out_shape=jax.ShapeDtypeStruct((b, s, gw), BF16),
        compiler_params=_params("parallel", "parallel"),
        name="mla_attn",
    )(q3, k3, v3)


def _mla_weights(w_uq, w_ukv):
    half = MLA_ROPE // 2
    hw = GROUP_HEADS * LANES
    wqa = jnp.zeros((256, hw), F32)
    wqb = jnp.zeros((256, hw), F32)
    wk = jnp.zeros((KV_LORA, hw), F32)
    wv = jnp.zeros((KV_LORA, GROUP_WIDTH), F32)
    for h in range(GROUP_HEADS):
        qh = w_uq[:, h * (MLA_NOPE + MLA_ROPE):(h + 1) * (MLA_NOPE + MLA_ROPE)]
        nope, r1, r2 = qh[:, :MLA_NOPE], qh[:, MLA_NOPE:MLA_NOPE + half], qh[:, MLA_NOPE + half:]
        o = h * LANES
        wqa = wqa.at[:Q_LORA, o:o + MLA_NOPE].set(nope)
        wqa = wqa.at[:Q_LORA, o + MLA_NOPE:o + MLA_NOPE + half].set(r1)
        wqa = wqa.at[:Q_LORA, o + MLA_NOPE + half:o + MLA_NOPE + 2 * half].set(r2)
        wqb = wqb.at[:Q_LORA, o + MLA_NOPE:o + MLA_NOPE + half].set(r2)
        wqb = wqb.at[:Q_LORA, o + MLA_NOPE + half:o + MLA_NOPE + 2 * half].set(r1)
        kvh = w_ukv[:, h * 2 * HEAD_DIM:(h + 1) * 2 * HEAD_DIM]
        wk = wk.at[:, o:o + MLA_NOPE].set(kvh[:, :MLA_NOPE])
        wv = wv.at[:, h * HEAD_DIM:(h + 1) * HEAD_DIM].set(kvh[:, MLA_NOPE:])
    return wqa.astype(BF16), wqb.astype(BF16), wk.astype(BF16), wv.astype(BF16)


def _mla_tables(s):
    half = MLA_ROPE // 2
    hw = GROUP_HEADS * LANES
    inv_freq = ROPE_THETA ** (-jnp.arange(half, dtype=F32) / half)
    ang = jnp.arange(s).astype(F32)[:, None] * inv_freq[None, :]
    c, sn = jnp.cos(ang), jnp.sin(ang)
    head_cos = jnp.concatenate([jnp.ones((s, MLA_NOPE), F32), c, c, jnp.zeros((s, LANES - MLA_NOPE - 2 * half), F32)], axis=1)
    head_sin = jnp.concatenate([jnp.zeros((s, MLA_NOPE), F32), -sn, sn, jnp.zeros((s, LANES - MLA_NOPE - 2 * half), F32)], axis=1)
    cos_t = jnp.concatenate([head_cos] * GROUP_HEADS, axis=1)
    sin_t = jnp.concatenate([head_sin] * GROUP_HEADS, axis=1)
    ea = np.zeros((128, hw), np.float32)
    eb = np.zeros((128, hw), np.float32)
    kr = KR_OFF - SMALL_OFF
    for h in range(GROUP_HEADS):
        o = h * LANES + MLA_NOPE
        for j in range(half):
            ea[kr + j, o + j] = 1.0
            ea[kr + half + j, o + half + j] = 1.0
            eb[kr + half + j, o + j] = 1.0
            eb[kr + j, o + half + j] = 1.0
    return cos_t, sin_t, jnp.asarray(ea, BF16), jnp.asarray(eb, BF16)


def _band_kernel(q_ref, k_ref, v_ref, bias_ref, o_ref, *, tq):
    left = LEFT_CHUNKS * CHUNK

    def tile(end):
        first = max(0, end - tq - left)
        w = end - first
        q = q_ref[...] * HEAD_DIM ** -0.5
        k, v = k_ref[first:end, :], v_ref[first:end, :]
        outs = []
        for h in range(GROUP_HEADS):
            sl = slice(h * HEAD_DIM, (h + 1) * HEAD_DIM)
            lg = _dot_nt(q[:, sl], k[:, sl]) + bias_ref[h, :, left + tq - w:]
            outs.append(_softmax_pv(lg, v[:, sl]))
        o_ref[...] = jnp.concatenate(outs, axis=1).astype(o_ref.dtype)

    _per_tile(tile, k_ref.shape[0] // tq, tq)


def _band_attn(qkv3, bias, tq=LANES):
    b, s, _ = qkv3.shape
    gw = GROUP_WIDTH
    return pl.pallas_call(
        functools.partial(_band_kernel, tq=tq),
        grid=(b, s // tq),
        in_specs=[pl.BlockSpec((None, tq, gw), lambda bi, i: (bi, i, 0)),
                  pl.BlockSpec((None, s, gw), lambda bi, i: (bi, 0, 1)),
                  pl.BlockSpec((None, s, gw), lambda bi, i: (bi, 0, 2)),
                  pl.BlockSpec(bias.shape, lambda bi, i: (0, 0, 0))],
        out_specs=pl.BlockSpec((None, tq, gw), lambda bi, i: (bi, i, 0)),
        out_shape=jax.ShapeDtypeStruct((b, s, gw), BF16),
        compiler_params=_params("parallel", "parallel"),
        name="band_attn",
    )(qkv3, qkv3, qkv3, bias)


def _band_bias(rel_table, tq=LANES):
    pad = LEFT_CHUNKS * CHUNK
    win = pad + tq
    i = np.arange(tq)[:, None]
    r = np.arange(win)[None, :]
    kc = r // CHUNK - LEFT_CHUNKS
    qc = i // CHUNK
    in_band = (kc <= qc) & (kc >= qc - LEFT_CHUNKS)
    diag = pad + tq - 1 - np.arange(tq + win - 1)
    bias = _toeplitz(rel_table[:, np.clip(diag, -MAX_REL, MAX_REL) + MAX_REL].astype(F32), tq, win)
    return jnp.where(jnp.asarray(in_band)[None], bias, NEG)


def _outproj_kernel(a_ref, b_ref, c_ref, d_ref, w_ref, x_ref, g_ref, bt_ref, wr_ref, rb_ref,
                    y_ref, gate_ref, *, alpha):
    acc = alpha * x_ref[...]
    for j, ref in enumerate((a_ref, b_ref, c_ref, d_ref)):
        acc = acc + _dot(ref[...], w_ref[j * GROUP_WIDTH:(j + 1) * GROUP_WIDTH, :])
    y = _layer_norm(acc, g_ref[...], bt_ref[...])
    y_ref[...] = y

    y_hi = y.astype(BF16)
    y_lo = (y - y_hi.astype(F32)).astype(BF16)
    logits = _dot_nt(wr_ref[0], y_hi) + (_dot_nt(wr_ref[0], y_lo) + _dot_nt(wr_ref[1], y_hi))
    score = 1.0 / (1.0 + jnp.exp(-logits))
    pick = score + rb_ref[...]
    row = lax.broadcasted_iota(jnp.int32, score.shape, 0).astype(F32)
    raw = jnp.zeros_like(score)
    for _ in range(TOP_K):
        best = jnp.max(pick, axis=0, keepdims=True)
        first = jnp.min(jnp.where(pick == best, row, float(LANES)), axis=0, keepdims=True)
        hit = row == first
        raw = jnp.where(hit, score, raw)
        pick = jnp.where(hit, -jnp.inf, pick)
    gates = (raw / jnp.sum(raw, axis=0, keepdims=True) * ROUTED_SCALE).T
    for grp in range(gate_ref.shape[0]):
        gate_ref[grp] = gates[:, grp * MOE_GROUP:(grp + 1) * MOE_GROUP]


def _outproj_ln_router(mix, w_out, x2, g, bt, w_router2, router_bias, alpha, tm=512):
    n = x2.shape[0]
    n_groups = N_EXPERTS // MOE_GROUP
    full = lambda a: pl.BlockSpec(a.shape, lambda i: (0,) * a.ndim)
    return pl.pallas_call(
        functools.partial(_outproj_kernel, alpha=alpha),
        grid=(n // tm,),
        in_specs=[pl.BlockSpec((tm, GROUP_WIDTH), lambda i: (i, 0))] * 4
                 + [full(w_out), pl.BlockSpec((tm, D_MODEL), lambda i: (i, 0)),
                    full(g), full(bt), full(w_router2), full(router_bias)],
        out_specs=[pl.BlockSpec((tm, D_MODEL), lambda i: (i, 0)),
                   pl.BlockSpec((n_groups, tm, MOE_GROUP), lambda i: (0, i, 0))],
        out_shape=[jax.ShapeDtypeStruct((n, D_MODEL), F32),
                   jax.ShapeDtypeStruct((n_groups, n, MOE_GROUP), F32)],
        compiler_params=_params("parallel"),
        name="outproj_ln_router",
    )(*mix, w_out, x2, g, bt, w_router2, router_bias)


def _moe_kernel(x_ref, gate_ref, wgu_ref, wd_ref, wsgu_ref, wsd_ref, g_ref, bt_ref, o_ref,
                acc_ref, xb_ref, *, alpha, group):
    e = pl.program_id(1)

    @pl.when(e == 0)
    def _():
        xb = x_ref[...].astype(BF16)
        xb_ref[...] = xb
        hs = _dot(xb, wsgu_ref[...])
        a = _silu(hs[:, :D_SHARED]) * hs[:, D_SHARED:]
        acc_ref[...] = _dot(a.astype(BF16), wsd_ref[...])

    xb = xb_ref[...]
    gates = gate_ref[...]
    acts = []
    for j in range(group):
        h = _dot(xb, wgu_ref[j])
        a = _silu(h[:, :D_EXPERT]) * h[:, D_EXPERT:] * gates[:, j:j + 1]
        acts.append(a.astype(BF16))
    w_down = wd_ref[...].reshape(group * D_EXPERT, D_MODEL)
    acc_ref[...] += _dot(jnp.concatenate(acts, axis=1), w_down)

    @pl.when(e == pl.num_programs(1) - 1)
    def _():
        y = alpha * x_ref[...] + acc_ref[...]
        o_ref[...] = _layer_norm(y, g_ref[...], bt_ref[...])


def _moe_ln(x2, gates_g, w_gu, w_down, w_sh_gu, w_sh_down, g, bt, layer, alpha, tm=1024):
    n = x2.shape[0]
    n_groups, _, group = gates_g.shape
    full = lambda a: pl.BlockSpec(a.shape, lambda i, e: (0,) * a.ndim)
    return pl.pallas_call(
        functools.partial(_moe_kernel, alpha=alpha, group=group),
        grid=(n // tm, n_groups),
        in_specs=[pl.BlockSpec((tm, D_MODEL), lambda i, e: (i, 0)),
                  pl.BlockSpec((None, tm, group), lambda i, e: (e, i, 0)),
                  pl.BlockSpec((None, group, D_MODEL, 2 * D_EXPERT), lambda i, e: (layer, e, 0, 0)),
                  pl.BlockSpec((None, group, D_EXPERT, D_MODEL), lambda i, e: (layer, e, 0, 0)),
                  full(w_sh_gu), full(w_sh_down), full(g), full(bt)],
        out_specs=pl.BlockSpec((tm, D_MODEL), lambda i, e: (i, 0)),
        out_shape=jax.ShapeDtypeStruct((n, D_MODEL), F32),
        scratch_shapes=[pltpu.VMEM((tm, D_MODEL), F32), pltpu.VMEM((tm, D_MODEL), BF16)],
        compiler_params=_params("parallel", "arbitrary"),
        name="moe_ln",
    )(x2, gates_g, w_gu, w_down, w_sh_gu, w_sh_down, g, bt)


def _inproj_weights(w_in):
    sizes = (3 * GROUP_WIDTH, GROUP_HEADS, 3 * GROUP_WIDTH, IDX_HEADS * IDX_DIM, IDX_DIM, IDX_HEADS,
             Q_LORA, KV_LORA, MLA_ROPE, 3 * GROUP_WIDTH)
    splits = tuple(int(v) for v in np.cumsum(sizes)[:-1])
    (fox, fox_f, dsa, idx_q, idx_k, idx_w, c_q, c_kv, k_r, band) = jnp.split(w_in, splits, axis=-1)
    lead = w_in.shape[:-1]
    misc = jnp.zeros(lead + (MISC_W,), w_in.dtype)
    for off, part in ((CQ_OFF, c_q), (CKV_OFF, c_kv), (FF_OFF, fox_f), (IW_OFF, idx_w), (IK_OFF, idx_k), (KR_OFF, k_r)):
        misc = misc.at[..., off:off + part.shape[-1]].set(part)
    dsa_qk, dsa_v = dsa[..., :2 * GROUP_WIDTH], dsa[..., 2 * GROUP_WIDTH:]
    w_re = jnp.concatenate([fox, dsa_qk, band, idx_q, misc], axis=-1).astype(BF16)
    w_t = jnp.concatenate([dsa_v, misc[..., SMALL_OFF:SMALL_OFF + SMALL_T_ROWS]], axis=-1)
    return w_re, jnp.swapaxes(w_t, -1, -2).astype(BF16)


def _pad_lanes(a, width, value=0.0):
    return jnp.pad(a, [(0, 0)] * (a.ndim - 1) + [(0, width - a.shape[-1])], constant_values=value)


def kernel(x, w_in, b_forget, mla_q_norm, mla_kv_norm, w_mla_uq, w_mla_ukv, t5_rel_bias, chunk_rel_bias,
           w_out, ln1_g, ln1_b, w_router, router_bias, w_exp_gu, w_exp_down, w_sh_gu, w_sh_down, ln2_g, ln2_b):
    b, s, d = x.shape
    depth = w_in.shape[0]
    n = b * s
    alpha = (2 * depth) ** 0.25

    w_in_re, w_in_t = _inproj_weights(w_in)
    t5_near = _t5_near(t5_rel_bias, DSA_TQ)
    cos_t, sin_t, ea, eb = _mla_tables(s)
    w_out_b = w_out.astype(BF16)
    w_gu_b, w_down_b = w_exp_gu.astype(BF16), w_exp_down.astype(BF16)
    w_sh_gu_b, w_sh_down_b = w_sh_gu.astype(BF16), w_sh_down.astype(BF16)
    w_router_t = jnp.swapaxes(_pad_lanes(w_router, LANES), -1, -2)
    w_router_hi = w_router_t.astype(BF16)
    w_router_lo = (w_router_t - w_router_hi.astype(F32)).astype(BF16)
    w_router2 = jnp.stack([w_router_hi, w_router_lo], axis=1)
    router_bias_p = _pad_lanes(router_bias, LANES, -jnp.inf)[..., None]

    x2 = x.reshape(n, d)
    for l in range(depth):
        fox, dsa_qk, band, iq, misc, dsa_vt, small_t = _inproj(x2, w_in_re[l], w_in_t[l], s)
        misc3 = misc.reshape(b, s, MISC_W)

        cum_row = _fox_cum(small_t, _pad_lanes(b_forget[l][None, :], 8).T)
        o_fox = _fox_attn(fox.reshape(b, s, -1), cum_row)

        o_dsa = _dsa_attn(dsa_qk.reshape(b, s, -1), dsa_vt, iq.reshape(b, s, -1), misc3, small_t, t5_near, DSA_TQ)

        wqa, wqb, wk, wv = _mla_weights(w_mla_uq[l], w_mla_ukv[l])
        q_m, k_m, v_m = _mla_prep(misc, _pad_lanes(mla_q_norm[l][None, :], 256), mla_kv_norm[l][None, :],
                                  wqa, wqb, wk, wv, ea, eb, cos_t, sin_t, s)
        o_mla = _mla_attn(q_m.reshape(b, s, -1), k_m.reshape(b, s, -1), v_m.reshape(b, s, -1))

        o_band = _band_attn(band.reshape(b, s, -1), _band_bias(chunk_rel_bias[l], BAND_TQ), BAND_TQ)

        mix = [o.reshape(n, GROUP_WIDTH) for o in (o_fox, o_dsa, o_mla, o_band)]
        x2, gates_g = _outproj_ln_router(mix, w_out_b[l], x2, ln1_g[l][None, :], ln1_b[l][None, :],
                                         w_router2[l], router_bias_p[l], alpha)
        x2 = _moe_ln(x2, gates_g, w_gu_b, w_down_b, w_sh_gu_b[l], w_sh_down_b[l],
                     ln2_g[l][None, :], ln2_b[l][None, :], l, alpha)
    return x2.reshape(b, s, d)
```

```python
import functools
import math

import numpy as np
import jax
import jax.numpy as jnp
from jax import lax
from jax.experimental import pallas as pl
from jax.experimental.pallas import tpu as pltpu

D_MODEL = 1024
CHUNK = 64
HEAD_DIM = 64
GROUP_HEADS = 4
GROUP_WIDTH = GROUP_HEADS * HEAD_DIM
IDX_HEADS = 8
IDX_DIM = 32
TOPK_MAX = 256
T5_BUCKETS = 32
T5_MAX_DIST = 128
Q_LORA = 192
KV_LORA = 128
MLA_NOPE = 64
MLA_ROPE = 32
ROPE_THETA = 10000.0
LEFT_CHUNKS = 8
MAX_REL = 256
N_EXPERTS = 64
TOP_K = 8
D_EXPERT = 256
D_SHARED = 256
ROUTED_SCALE = 2.5
MOE_GROUP = 4
BAND_TQ = 256
DSA_TQ = 128
DSA_BUCKET = 256
DSA_NEAR = DSA_BUCKET + T5_MAX_DIST
LN_EPS = 1e-5
RMS_EPS = 1e-6

LANES = 128
NEG = -1e30
BISECT_STEPS = 16
VMEM_LIMIT = 56 * 1024 * 1024

MISC_W = 512
CQ_OFF, CKV_OFF, SMALL_OFF = 0, 256, 384
FF_OFF, IW_OFF, IK_OFF, KR_OFF = 384, 392, 416, 448
SMALL_T_ROWS = 16

BF16 = jnp.bfloat16
F32 = jnp.float32


def _dot(a, b):
    return jnp.dot(a, b, preferred_element_type=F32)


def _dot_nt(a, b):
    return lax.dot_general(a, b, (((1,), (1,)), ((), ())), preferred_element_type=F32)


def _params(*sem):
    return pltpu.CompilerParams(dimension_semantics=sem, vmem_limit_bytes=VMEM_LIMIT)


def _layer_norm(y, g, b):
    mu = jnp.mean(y, axis=-1, keepdims=True)
    yc = y - mu
    var = jnp.mean(yc * yc, axis=-1, keepdims=True)
    return yc * lax.rsqrt(var + LN_EPS) * g + b


def _chunk_end(pos):
    return (jnp.right_shift(pos, CHUNK.bit_length() - 1) + 1) * CHUNK


def _silu(x):
    return x / (1.0 + jnp.exp(-x))


def _softmax_pv(s, v):
    m = jnp.max(s, axis=1, keepdims=True)
    p = jnp.exp(s - m)
    l = jnp.sum(p, axis=1, keepdims=True)
    return _dot(p.astype(BF16), v) / l


def _inproj_kernel(x_ref, w_ref, wt_ref, fox_ref, dsa_ref, band_ref, iq_ref, misc_ref, vt_ref, small_t_ref):
    xb = x_ref[...].astype(BF16)
    off = 0
    for ref in (fox_ref, dsa_ref, band_ref, iq_ref, misc_ref):
        w = ref.shape[1]
        ref[...] = _dot(xb, w_ref[:, off:off + w]).astype(ref.dtype)
        off += w
    t = _dot_nt(wt_ref[...], xb)
    vt_ref[...] = t[:GROUP_WIDTH, :].astype(vt_ref.dtype)
    small_t_ref[...] = t[GROUP_WIDTH:, :]


def _inproj(x2, w_re, w_t, s, tm=512):
    n = x2.shape[0]
    per_seq = s // tm
    widths = (3 * GROUP_WIDTH, 2 * GROUP_WIDTH, 3 * GROUP_WIDTH, IDX_HEADS * IDX_DIM, MISC_W)
    dtypes = (BF16, BF16, BF16, BF16, F32)
    t_rows = (GROUP_WIDTH, SMALL_T_ROWS)
    t_dtypes = (BF16, F32)
    return pl.pallas_call(
        _inproj_kernel,
        grid=(n // tm,),
        in_specs=[pl.BlockSpec((tm, D_MODEL), lambda i: (i, 0)),
                  pl.BlockSpec(w_re.shape, lambda i: (0, 0)),
                  pl.BlockSpec(w_t.shape, lambda i: (0, 0))],
        out_specs=[pl.BlockSpec((tm, w), lambda i: (i, 0)) for w in widths]
                  + [pl.BlockSpec((None, r, tm), lambda i: (i // per_seq, 0, i % per_seq)) for r in t_rows],
        out_shape=[jax.ShapeDtypeStruct((n, w), d) for w, d in zip(widths, dtypes)]
                  + [jax.ShapeDtypeStruct((n // s, r, s), d) for r, d in zip(t_rows, t_dtypes)],
        compiler_params=_params("parallel"),
        name="inproj",
    )(x2, w_re, w_t)


def _cum_kernel(f_ref, b_ref, o_ref):
    x = f_ref[...] + b_ref[...]
    acc = jnp.minimum(x, 0.0) - jnp.log(1.0 + jnp.exp(-jnp.abs(x)))
    s = x.shape[1]
    lane = lax.broadcasted_iota(jnp.int32, x.shape, 1)
    d = 1
    while d < s:
        acc = acc + jnp.where(lane >= d, pltpu.roll(acc, d, 1), 0.0)
        d *= 2
    o_ref[...] = acc


def _fox_cum(f_t, b_col):
    b, _, s = f_t.shape
    r = b_col.shape[0]
    return pl.pallas_call(
        _cum_kernel,
        grid=(b,),
        in_specs=[pl.BlockSpec((None, r, s), lambda i: (i, 0, 0)),
                  pl.BlockSpec((r, 1), lambda i: (0, 0))],
        out_specs=pl.BlockSpec((None, r, s), lambda i: (i, 0, 0)),
        out_shape=jax.ShapeDtypeStruct((b, r, s), F32),
        compiler_params=_params("parallel"),
        name="fox_cum",
    )(f_t, b_col)


def _mask_last_block(lg, ok):
    tq = ok.shape[1]
    w = lg.shape[1]
    last = jnp.where(ok, lg[:, w - tq:], NEG)
    return last if w == tq else jnp.concatenate([lg[:, :w - tq], last], axis=1)


def _per_tile(body, n_tiles, tq):
    i = pl.program_id(1)
    for j in range(n_tiles):
        pl.when(i == j)(functools.partial(body, (j + 1) * tq))


def _fox_kernel(q_ref, k_ref, v_ref, ck_ref, o_ref, *, tq):
    def tile(w):
        q = q_ref[...] * HEAD_DIM ** -0.5
        k, v, ck = k_ref[:w, :], v_ref[:w, :], ck_ref[:, :w]
        ok = (lax.broadcasted_iota(jnp.int32, (tq, tq), 1) <= lax.broadcasted_iota(jnp.int32, (tq, tq), 0))
        outs = []
        for h in range(GROUP_HEADS):
            sl = slice(h * HEAD_DIM, (h + 1) * HEAD_DIM)
            lg = _dot_nt(q[:, sl], k[:, sl]) - ck[h:h + 1, :]
            outs.append(_softmax_pv(_mask_last_block(lg, ok), v[:, sl]))
        o_ref[...] = jnp.concatenate(outs, axis=1).astype(o_ref.dtype)

    _per_tile(tile, k_ref.shape[0] // tq, tq)


def _fox_attn(qkv3, cum_row, tq=256):
    b, s, _ = qkv3.shape
    gw = GROUP_WIDTH
    return pl.pallas_call(
        functools.partial(_fox_kernel, tq=tq),
        grid=(b, s // tq),
        in_specs=[pl.BlockSpec((None, tq, gw), lambda bi, i: (bi, i, 0)),
                  pl.BlockSpec((None, s, gw), lambda bi, i: (bi, 0, 1)),
                  pl.BlockSpec((None, s, gw), lambda bi, i: (bi, 0, 2)),
                  pl.BlockSpec((None, 8, s), lambda bi, i: (bi, 0, 0))],
        out_specs=pl.BlockSpec((None, tq, gw), lambda bi, i: (bi, i, 0)),
        out_shape=jax.ShapeDtypeStruct((b, s, gw), BF16),
        compiler_params=_params("parallel", "parallel"),
        name="fox_attn",
    )(qkv3, qkv3, qkv3, cum_row)


def _over_keys(reduce, x, chains=4):
    w, tq = x.shape
    sub = 8
    part = reduce(x.reshape(chains, w // (sub * chains), sub, tq), axis=1)
    return reduce(reduce(part, axis=0), axis=0, keepdims=True)


def _kth_largest(sc_ref, w, lo, hi, searching, kf):
    def count_ge(t):
        return _over_keys(jnp.sum, jnp.where(sc_ref[:w, :] >= t, 1.0, 0.0))

    def halve(_, bracket):
        lo, hi = bracket
        mid = lo + (hi - lo) * 0.5
        more = count_ge(mid) >= kf
        return jnp.where(more, mid, lo), jnp.where(more, hi, mid)

    _, hi = lax.fori_loop(0, BISECT_STEPS, halve, (lo, hi))

    def unresolved(state):
        hi, thr, n_ge, open_, it = state
        return jnp.logical_and(jnp.max(open_) > 0.0, it <= w)

    def peel(state):
        hi, thr, n_ge, open_, it = state
        sc = sc_ref[:w, :]
        cand = _over_keys(jnp.max, jnp.where(sc < hi, sc, -jnp.inf))
        c = count_ge(cand)
        found = (open_ > 0.0) & (c >= kf)
        thr = jnp.where(found, cand, thr)
        n_ge = jnp.where(found, c, n_ge)
        open_ = jnp.where(found, 0.0, open_)
        return jnp.where(open_ > 0.0, cand, hi), thr, n_ge, open_, it + 1

    state = (hi, jnp.full_like(lo, jnp.finfo(F32).min), jnp.full_like(lo, kf),
             jnp.where(searching, 1.0, 0.0), jnp.int32(0))
    _, thr, n_ge, _, _ = lax.while_loop(unresolved, peel, state)
    return thr, n_ge


def _dsa_tile(w, q_ref, k_ref, vt_ref, iq_ref, ik_ref, iwt_ref, t5_ref, o_ref, sc_ref, *, tq, n_sel):
    m = pl.program_id(1)
    limit = _chunk_end(m * tq + lax.broadcasted_iota(jnp.int32, (1, tq), 1))
    valid = lax.broadcasted_iota(jnp.int32, (w, tq), 0) < limit

    iq = iq_ref[...]
    ik = ik_ref[:w, :]
    iw = iwt_ref[...] * (IDX_HEADS ** -0.5 * IDX_DIM ** -0.5)
    score = jnp.zeros((w, tq), F32)
    for h in range(IDX_HEADS):
        d = _dot_nt(ik, iq[:, h * IDX_DIM:(h + 1) * IDX_DIM])
        score = score + jnp.maximum(d, 0.0) * iw[h:h + 1, :]
    score = jnp.where(valid, score, -jnp.inf)
    sc_ref[:w, :] = score

    kf = float(n_sel)
    thr = jnp.full((1, tq), jnp.finfo(F32).min, F32)
    if w > n_sel:
        searching = limit > n_sel
        top = _over_keys(jnp.max, score)
        lo0 = jnp.where(searching, _over_keys(jnp.min, jnp.where(valid, score, jnp.inf)), 0.0)
        hi0 = jnp.where(searching, top + jnp.abs(top) * 2.0 ** -20 + 1e-37, 0.0)
        thr, n_ge = _kth_largest(sc_ref, w, lo0, hi0, searching, kf)

        tie_split = jnp.max(jnp.where(searching & (n_ge > kf), 1.0, 0.0)) > 0.0

        @pl.when(tie_split)
        def _drop_surplus_ties():
            score = sc_ref[:w, :]
            room = kf - _over_keys(jnp.sum, jnp.where(score > thr, 1.0, 0.0))
            blk = DSA_BUCKET
            tri = (lax.broadcasted_iota(jnp.int32, (blk, blk), 1)
                   <= lax.broadcasted_iota(jnp.int32, (blk, blk), 0)).astype(BF16)
            before = jnp.zeros((1, tq), F32)
            for c in range(w // blk):
                sb = score[c * blk:(c + 1) * blk, :]
                eb = jnp.where(sb == thr, 1.0, 0.0)
                inc = _dot(tri, eb.astype(BF16)) + before
                surplus = (eb > 0.0) & (inc - eb >= room)
                sc_ref[c * blk:(c + 1) * blk, :] = jnp.where(surplus, -jnp.inf, sb)
                before = before + jnp.sum(eb, axis=0, keepdims=True)

    mask = sc_ref[:w, :] >= thr

    near = DSA_NEAR
    shift = (w - (m + 1) * tq) // tq
    qs = q_ref[...] * HEAD_DIM ** -0.5
    k = k_ref[:w, :]
    outs = []
    for h in range(GROUP_HEADS):
        sl = slice(h * HEAD_DIM, (h + 1) * HEAD_DIM)
        lg = _dot_nt(k[:, sl], qs[:, sl])
        tab = t5_ref[h, shift]
        if w < near:
            lg = lg + tab[near - w:, :]
        else:
            biased = lg[w - near:, :] + tab
            lg = biased if w == near else jnp.concatenate([lg[:w - near, :], biased], axis=0)
        lg = jnp.where(mask, lg, NEG)
        p = jnp.exp(lg - _over_keys(jnp.max, lg))
        l = _over_keys(jnp.sum, p)
        outs.append(_dot(vt_ref[sl, :w], p.astype(BF16)) / l)
    o_ref[...] = jnp.concatenate(outs, axis=0).T.astype(o_ref.dtype)


def _dsa_kernel(*refs, tq, n_sel):
    m = pl.program_id(1)
    s = refs[-1].shape[0]
    per = DSA_BUCKET // tq
    for j in range(s // DSA_BUCKET):
        pl.when(m // per == j)(functools.partial(_dsa_tile, (j + 1) * DSA_BUCKET, *refs, tq=tq, n_sel=n_sel))


def _dsa_attn(qk3, v_t, iq3, ik3, small_t, t5_near, tq):
    b, s, _ = qk3.shape
    gw = GROUP_WIDTH
    n_sel = min(TOPK_MAX, s // 4)
    return pl.pallas_call(
        functools.partial(_dsa_kernel, tq=tq, n_sel=n_sel),
        grid=(b, s // tq),
        in_specs=[pl.BlockSpec((None, tq, gw), lambda bi, i: (bi, i, 0)),
                  pl.BlockSpec((None, s, gw), lambda bi, i: (bi, 0, 1)),
                  pl.BlockSpec((None, gw, s), lambda bi, i: (bi, 0, 0)),
                  pl.BlockSpec((None, tq, IDX_HEADS * IDX_DIM), lambda bi, i: (bi, i, 0)),
                  pl.BlockSpec((None, s, IDX_DIM), lambda bi, i: (bi, 0, 0)),
                  pl.BlockSpec((None, IDX_HEADS, tq), lambda bi, i: (bi, 1, i)),
                  pl.BlockSpec(t5_near.shape, lambda bi, i: (0, 0, 0, 0))],
        out_specs=pl.BlockSpec((None, tq, gw), lambda bi, i: (bi, i, 0)),
        out_shape=jax.ShapeDtypeStruct((b, s, gw), BF16),
        scratch_shapes=[pltpu.VMEM((s, tq), F32)],
        compiler_params=_params("parallel", "parallel"),
        name="dsa_attn",
    )(qk3, qk3, v_t, iq3, ik3, small_t, t5_near)


def _toeplitz(w, rows, cols):
    length = rows + cols - 1
    wp = jnp.pad(w, ((0, 0), (0, 1)))
    flat = jnp.tile(wp, (1, rows))[:, :rows * length]
    return flat.reshape(w.shape[0], rows, length)[:, :, rows - 1:rows - 1 + cols]


def _t5_bucket(rel):
    half = T5_BUCKETS // 2
    max_exact = half // 2
    n = jnp.abs(rel)
    large = max_exact + (jnp.log(jnp.maximum(n, 1).astype(F32) / max_exact)
                         / math.log(T5_MAX_DIST / max_exact) * (half - max_exact)).astype(jnp.int32)
    large = jnp.minimum(large, half - 1)
    return jnp.where(rel > 0, half, 0) + jnp.where(n < max_exact, n, large)


def _t5_near(t5_rel_bias, tq):
    assert tq >= T5_MAX_DIST and DSA_BUCKET % tq == 0
    near = DSA_NEAR
    far = t5_rel_bias[T5_BUCKETS // 2 - 1]
    tables = []
    for shift in range(DSA_BUCKET // tq):
        rel = (1 + shift) * tq - 1 - np.arange(near + tq - 1, dtype=np.int32)
        vals = jnp.stack([t5_rel_bias[:, h][_t5_bucket(jnp.asarray(rel))] for h in range(GROUP_HEADS)])
        tables.append(_toeplitz((vals - far[:, None]).astype(F32), near, tq))
    return jnp.stack(tables, axis=1)


def _mla_prep_kernel(misc_ref, qn_ref, kvn_ref, wqa_ref, wqb_ref, wk_ref, wv_ref, ea_ref, eb_ref,
                     cos_ref, sin_ref, q_ref, k_ref, v_ref):
    misc = misc_ref[...]
    cq = misc[:, CQ_OFF:CQ_OFF + 256]
    ckv = misc[:, CKV_OFF:CKV_OFF + KV_LORA]
    small = misc[:, SMALL_OFF:SMALL_OFF + 128]
    cqn = cq * lax.rsqrt(jnp.sum(cq * cq, axis=1, keepdims=True) / Q_LORA + RMS_EPS) * qn_ref[...]
    ckvn = ckv * lax.rsqrt(jnp.mean(ckv * ckv, axis=1, keepdims=True) + RMS_EPS) * kvn_ref[...]
    cqb, ckvb = cqn.astype(BF16), ckvn.astype(BF16)
    cos, sin = cos_ref[...], sin_ref[...]
    q = (_dot(cqb, wqa_ref[...]) * cos + _dot(cqb, wqb_ref[...]) * sin) * (MLA_NOPE + MLA_ROPE) ** -0.5
    s_hi = small.astype(BF16)
    s_lo = (small - s_hi.astype(F32)).astype(BF16)
    kr_a = _dot(s_hi, ea_ref[...]) + _dot(s_lo, ea_ref[...])
    kr_b = _dot(s_hi, eb_ref[...]) + _dot(s_lo, eb_ref[...])
    k = (_dot(ckvb, wk_ref[...]) + kr_a) * cos + kr_b * sin
    q_ref[...] = q.astype(BF16)
    k_ref[...] = k.astype(BF16)
    v_ref[...] = _dot(ckvb, wv_ref[...]).astype(BF16)


def _mla_prep(misc, qn, kvn, wqa, wqb, wk, wv, ea, eb, cos_t, sin_t, s, tm=512):
    n = misc.shape[0]
    pos_blocks = s // tm
    full = lambda a: pl.BlockSpec(a.shape, lambda i: (0,) * a.ndim)
    hw = GROUP_HEADS * LANES
    return pl.pallas_call(
        _mla_prep_kernel,
        grid=(n // tm,),
        in_specs=[pl.BlockSpec((tm, MISC_W), lambda i: (i, 0)),
                  full(qn), full(kvn), full(wqa), full(wqb), full(wk), full(wv), full(ea), full(eb),
                  pl.BlockSpec((tm, hw), lambda i: (i % pos_blocks, 0)),
                  pl.BlockSpec((tm, hw), lambda i: (i % pos_blocks, 0))],
        out_specs=[pl.BlockSpec((tm, hw), lambda i: (i, 0)),
                   pl.BlockSpec((tm, hw), lambda i: (i, 0)),
                   pl.BlockSpec((tm, GROUP_WIDTH), lambda i: (i, 0))],
        out_shape=[jax.ShapeDtypeStruct((n, hw), BF16),
                   jax.ShapeDtypeStruct((n, hw), BF16),
                   jax.ShapeDtypeStruct((n, GROUP_WIDTH), BF16)],
        compiler_params=_params("parallel"),
        name="mla_prep",
    )(misc, qn, kvn, wqa, wqb, wk, wv, ea, eb, cos_t, sin_t)


def _mla_kernel(q_ref, k_ref, v_ref, o_ref, *, tq):
    def tile(w):
        q, k, v = q_ref[...], k_ref[:w, :], v_ref[:w, :]
        row = lax.broadcasted_iota(jnp.int32, (tq, tq), 0)
        col = lax.broadcasted_iota(jnp.int32, (tq, tq), 1)
        ok = col < _chunk_end(row)
        outs = []
        for h in range(GROUP_HEADS):
            hl = slice(h * LANES, (h + 1) * LANES)
            lg = _mask_last_block(_dot_nt(q[:, hl], k[:, hl]), ok)
            outs.append(_softmax_pv(lg, v[:, h * HEAD_DIM:(h + 1) * HEAD_DIM]))
        o_ref[...] = jnp.concatenate(outs, axis=1).astype(o_ref.dtype)

    _per_tile(tile, k_ref.shape[0] // tq, tq)


def _mla_attn(q3, k3, v3, tq=256):
    b, s, hw = q3.shape
    gw = GROUP_WIDTH
    return pl.pallas_call(
        functools.partial(_mla_kernel, tq=tq),
        grid=(b, s // tq),
        in_specs=[pl.BlockSpec((None, tq, hw), lambda bi, i: (bi, i, 0)),
                  pl.BlockSpec((None, s, hw), lambda bi, i: (bi, 0, 0)),
                  pl.BlockSpec((None, s, gw), lambda bi, i: (bi, 0, 0))],
        out_specs=pl.BlockSpec((None, tq, gw), lambda bi, i: (bi, i, 0)),
        out_shape=jax.ShapeDtypeStruct((b, s, gw), BF16),
        compiler_params=_params("parallel", "parallel"),
        name="mla_attn",
    )(q3, k3, v3)


def _mla_weights(w_uq, w_ukv):
    depth = w_uq.shape[0]
    half = MLA_ROPE // 2
    q = w_uq.reshape(depth, Q_LORA, GROUP_HEADS, MLA_NOPE + MLA_ROPE)
    nope, r1, r2 = q[..., :MLA_NOPE], q[..., MLA_NOPE:MLA_NOPE + half], q[..., MLA_NOPE + half:]
    tail = jnp.zeros(q.shape[:-1] + (LANES - MLA_NOPE - 2 * half,), q.dtype)

    def q_rows(parts):
        a = jnp.concatenate(parts, axis=-1).reshape(depth, Q_LORA, GROUP_HEADS * LANES)
        return jnp.pad(a, ((0, 0), (0, 256 - Q_LORA), (0, 0)))

    wqa = q_rows([nope, r1, r2, tail])
    wqb = q_rows([jnp.zeros_like(nope), r2, r1, tail])
    kv = w_ukv.reshape(depth, KV_LORA, GROUP_HEADS, 2 * HEAD_DIM)
    k_nope, v = kv[..., :MLA_NOPE], kv[..., MLA_NOPE:]
    wk = jnp.concatenate([k_nope, jnp.zeros(k_nope.shape[:-1] + (LANES - MLA_NOPE,), kv.dtype)], axis=-1)
    wk = wk.reshape(depth, KV_LORA, GROUP_HEADS * LANES)
    wv = v.reshape(depth, KV_LORA, GROUP_WIDTH)
    return wqa.astype(BF16), wqb.astype(BF16), wk.astype(BF16), wv.astype(BF16)


def _mla_tables(s):
    half = MLA_ROPE // 2
    hw = GROUP_HEADS * LANES
    inv_freq = ROPE_THETA ** (-jnp.arange(half, dtype=F32) / half)
    ang = jnp.arange(s).astype(F32)[:, None] * inv_freq[None, :]
    c, sn = jnp.cos(ang), jnp.sin(ang)
    head_cos = jnp.concatenate([jnp.ones((s, MLA_NOPE), F32), c, c, jnp.zeros((s, LANES - MLA_NOPE - 2 * half), F32)], axis=1)
    head_sin = jnp.concatenate([jnp.zeros((s, MLA_NOPE), F32), -sn, sn, jnp.zeros((s, LANES - MLA_NOPE - 2 * half), F32)], axis=1)
    cos_t = jnp.concatenate([head_cos] * GROUP_HEADS, axis=1)
    sin_t = jnp.concatenate([head_sin] * GROUP_HEADS, axis=1)
    ea = np.zeros((128, hw), np.float32)
    eb = np.zeros((128, hw), np.float32)
    kr = KR_OFF - SMALL_OFF
    for h in range(GROUP_HEADS):
        o = h * LANES + MLA_NOPE
        for j in range(half):
            ea[kr + j, o + j] = 1.0
            ea[kr + half + j, o + half + j] = 1.0
            eb[kr + half + j, o + j] = 1.0
            eb[kr + j, o + half + j] = 1.0
    return cos_t, sin_t, jnp.asarray(ea, BF16), jnp.asarray(eb, BF16)


def _band_kernel(q_ref, k_ref, v_ref, bias_ref, o_ref, *, tq):
    left = LEFT_CHUNKS * CHUNK

    def tile(end):
        first = max(0, end - tq - left)
        w = end - first
        q = q_ref[...] * HEAD_DIM ** -0.5
        k, v = k_ref[first:end, :], v_ref[first:end, :]
        outs = []
        for h in range(GROUP_HEADS):
            sl = slice(h * HEAD_DIM, (h + 1) * HEAD_DIM)
            lg = _dot_nt(q[:, sl], k[:, sl]) + bias_ref[h, :, left + tq - w:]
            outs.append(_softmax_pv(lg, v[:, sl]))
        o_ref[...] = jnp.concatenate(outs, axis=1).astype(o_ref.dtype)

    _per_tile(tile, k_ref.shape[0] // tq, tq)


def _band_attn(qkv3, bias, tq=LANES):
    b, s, _ = qkv3.shape
    gw = GROUP_WIDTH
    return pl.pallas_call(
        functools.partial(_band_kernel, tq=tq),
        grid=(b, s // tq),
        in_specs=[pl.BlockSpec((None, tq, gw), lambda bi, i: (bi, i, 0)),
                  pl.BlockSpec((None, s, gw), lambda bi, i: (bi, 0, 1)),
                  pl.BlockSpec((None, s, gw), lambda bi, i: (bi, 0, 2)),
                  pl.BlockSpec(bias.shape, lambda bi, i: (0, 0, 0))],
        out_specs=pl.BlockSpec((None, tq, gw), lambda bi, i: (bi, i, 0)),
        out_shape=jax.ShapeDtypeStruct((b, s, gw), BF16),
        compiler_params=_params("parallel", "parallel"),
        name="band_attn",
    )(qkv3, qkv3, qkv3, bias)


def _band_bias(rel_table, tq=LANES):
    pad = LEFT_CHUNKS * CHUNK
    win = pad + tq
    i = np.arange(tq)[:, None]
    r = np.arange(win)[None, :]
    kc = r // CHUNK - LEFT_CHUNKS
    qc = i // CHUNK
    in_band = (kc <= qc) & (kc >= qc - LEFT_CHUNKS)
    diag = pad + tq - 1 - np.arange(tq + win - 1)
    bias = _toeplitz(rel_table[:, np.clip(diag, -MAX_REL, MAX_REL) + MAX_REL].astype(F32), tq, win)
    return jnp.where(jnp.asarray(in_band)[None], bias, NEG)


def _outproj_kernel(a_ref, b_ref, c_ref, d_ref, w_ref, x_ref, g_ref, bt_ref, wr_ref, rb_ref,
                    y_ref, gate_ref, *, alpha):
    acc = alpha * x_ref[...]
    for j, ref in enumerate((a_ref, b_ref, c_ref, d_ref)):
        acc = acc + _dot(ref[...], w_ref[j * GROUP_WIDTH:(j + 1) * GROUP_WIDTH, :])
    y = _layer_norm(acc, g_ref[...], bt_ref[...])
    y_ref[...] = y

    y_hi = y.astype(BF16)
    y_lo = (y - y_hi.astype(F32)).astype(BF16)
    logits = _dot_nt(wr_ref[0], y_hi) + (_dot_nt(wr_ref[0], y_lo) + _dot_nt(wr_ref[1], y_hi))
    score = 1.0 / (1.0 + jnp.exp(-logits))
    pick = score + rb_ref[...]
    row = lax.broadcasted_iota(jnp.int32, score.shape, 0).astype(F32)
    raw = jnp.zeros_like(score)
    for _ in range(TOP_K):
        best = jnp.max(pick, axis=0, keepdims=True)
        first = jnp.min(jnp.where(pick == best, row, float(LANES)), axis=0, keepdims=True)
        hit = row == first
        raw = jnp.where(hit, score, raw)
        pick = jnp.where(hit, -jnp.inf, pick)
    gates = (raw / jnp.sum(raw, axis=0, keepdims=True) * ROUTED_SCALE).T
    for grp in range(gate_ref.shape[0]):
        gate_ref[grp] = gates[:, grp * MOE_GROUP:(grp + 1) * MOE_GROUP]


def _outproj_ln_router(mix, w_out, x2, g, bt, w_router2, router_bias, alpha, tm=512):
    n = x2.shape[0]
    n_groups = N_EXPERTS // MOE_GROUP
    full = lambda a: pl.BlockSpec(a.shape, lambda i: (0,) * a.ndim)
    return pl.pallas_call(
        functools.partial(_outproj_kernel, alpha=alpha),
        grid=(n // tm,),
        in_specs=[pl.BlockSpec((tm, GROUP_WIDTH), lambda i: (i, 0))] * 4
                 + [full(w_out), pl.BlockSpec((tm, D_MODEL), lambda i: (i, 0)),
                    full(g), full(bt), full(w_router2), full(router_bias)],
        out_specs=[pl.BlockSpec((tm, D_MODEL), lambda i: (i, 0)),
                   pl.BlockSpec((n_groups, tm, MOE_GROUP), lambda i: (0, i, 0))],
        out_shape=[jax.ShapeDtypeStruct((n, D_MODEL), F32),
                   jax.ShapeDtypeStruct((n_groups, n, MOE_GROUP), F32)],
        compiler_params=_params("parallel"),
        name="outproj_ln_router",
    )(*mix, w_out, x2, g, bt, w_router2, router_bias)


def _moe_kernel(x_ref, gate_ref, wgu_ref, wd_ref, wsgu_ref, wsd_ref, g_ref, bt_ref, o_ref,
                acc_ref, xb_ref, *, alpha, group):
    e = pl.program_id(1)

    @pl.when(e == 0)
    def _():
        xb = x_ref[...].astype(BF16)
        xb_ref[...] = xb
        hs = _dot(xb, wsgu_ref[...])
        a = _silu(hs[:, :D_SHARED]) * hs[:, D_SHARED:]
        acc_ref[...] = _dot(a.astype(BF16), wsd_ref[...])

    xb = xb_ref[...]
    gates = gate_ref[...]
    acts = []
    for j in range(group):
        h = _dot(xb, wgu_ref[j])
        a = _silu(h[:, :D_EXPERT]) * h[:, D_EXPERT:] * gates[:, j:j + 1]
        acts.append(a.astype(BF16))
    w_down = wd_ref[...].reshape(group * D_EXPERT, D_MODEL)
    acc_ref[...] += _dot(jnp.concatenate(acts, axis=1), w_down)

    @pl.when(e == pl.num_programs(1) - 1)
    def _():
        y = alpha * x_ref[...] + acc_ref[...]
        o_ref[...] = _layer_norm(y, g_ref[...], bt_ref[...])


def _moe_ln(x2, gates_g, w_gu, w_down, w_sh_gu, w_sh_down, g, bt, layer, alpha, tm=1024):
    n = x2.shape[0]
    n_groups, _, group = gates_g.shape
    full = lambda a: pl.BlockSpec(a.shape, lambda i, e: (0,) * a.ndim)
    return pl.pallas_call(
        functools.partial(_moe_kernel, alpha=alpha, group=group),
        grid=(n // tm, n_groups),
        in_specs=[pl.BlockSpec((tm, D_MODEL), lambda i, e: (i, 0)),
                  pl.BlockSpec((None, tm, group), lambda i, e: (e, i, 0)),
                  pl.BlockSpec((None, group, D_MODEL, 2 * D_EXPERT), lambda i, e: (layer, e, 0, 0)),
                  pl.BlockSpec((None, group, D_EXPERT, D_MODEL), lambda i, e: (layer, e, 0, 0)),
                  full(w_sh_gu), full(w_sh_down), full(g), full(bt)],
        out_specs=pl.BlockSpec((tm, D_MODEL), lambda i, e: (i, 0)),
        out_shape=jax.ShapeDtypeStruct((n, D_MODEL), F32),
        scratch_shapes=[pltpu.VMEM((tm, D_MODEL), F32), pltpu.VMEM((tm, D_MODEL), BF16)],
        compiler_params=_params("parallel", "arbitrary"),
        name="moe_ln",
    )(x2, gates_g, w_gu, w_down, w_sh_gu, w_sh_down, g, bt)


def _inproj_weights(w_in):
    sizes = (3 * GROUP_WIDTH, GROUP_HEADS, 3 * GROUP_WIDTH, IDX_HEADS * IDX_DIM, IDX_DIM, IDX_HEADS,
             Q_LORA, KV_LORA, MLA_ROPE, 3 * GROUP_WIDTH)
    splits = tuple(int(v) for v in np.cumsum(sizes)[:-1])
    (fox, fox_f, dsa, idx_q, idx_k, idx_w, c_q, c_kv, k_r, band) = jnp.split(w_in, splits, axis=-1)
    pieces, at = [], 0
    for off, part in ((CQ_OFF, c_q), (CKV_OFF, c_kv), (FF_OFF, fox_f), (IW_OFF, idx_w), (IK_OFF, idx_k),
                      (KR_OFF, k_r), (MISC_W, None)):
        pieces.append(jnp.zeros(w_in.shape[:-1] + (off - at,), w_in.dtype))
        if part is not None:
            pieces.append(part)
            at = off + part.shape[-1]
    misc = jnp.concatenate(pieces, axis=-1)
    dsa_qk, dsa_v = dsa[..., :2 * GROUP_WIDTH], dsa[..., 2 * GROUP_WIDTH:]
    w_re = jnp.concatenate([fox, dsa_qk, band, idx_q, misc], axis=-1).astype(BF16)
    w_t = jnp.concatenate([dsa_v, misc[..., SMALL_OFF:SMALL_OFF + SMALL_T_ROWS]], axis=-1)
    return w_re, jnp.swapaxes(w_t, -1, -2).astype(BF16)


def _pad_lanes(a, width, value=0.0):
    return jnp.pad(a, [(0, 0)] * (a.ndim - 1) + [(0, width - a.shape[-1])], constant_values=value)


def kernel(x, w_in, b_forget, mla_q_norm, mla_kv_norm, w_mla_uq, w_mla_ukv, t5_rel_bias, chunk_rel_bias,
           w_out, ln1_g, ln1_b, w_router, router_bias, w_exp_gu, w_exp_down, w_sh_gu, w_sh_down, ln2_g, ln2_b):
    b, s, d = x.shape
    depth = w_in.shape[0]
    n = b * s
    alpha = (2 * depth) ** 0.25

    w_in_re, w_in_t = _inproj_weights(w_in)
    t5_near = _t5_near(t5_rel_bias, DSA_TQ)
    cos_t, sin_t, ea, eb = _mla_tables(s)
    wqa, wqb, wk, wv = _mla_weights(w_mla_uq, w_mla_ukv)
    w_out_b = w_out.astype(BF16)
    w_gu_b, w_down_b = w_exp_gu.astype(BF16), w_exp_down.astype(BF16)
    w_sh_gu_b, w_sh_down_b = w_sh_gu.astype(BF16), w_sh_down.astype(BF16)
    w_router_t = jnp.swapaxes(_pad_lanes(w_router, LANES), -1, -2)
    w_router_hi = w_router_t.astype(BF16)
    w_router_lo = (w_router_t - w_router_hi.astype(F32)).astype(BF16)
    w_router2 = jnp.stack([w_router_hi, w_router_lo], axis=1)
    router_bias_p = _pad_lanes(router_bias, LANES, -jnp.inf)[..., None]

    x2 = x.reshape(n, d)
    for l in range(depth):
        fox, dsa_qk, band, iq, misc, dsa_vt, small_t = _inproj(x2, w_in_re[l], w_in_t[l], s)
        misc3 = misc.reshape(b, s, MISC_W)

        cum_row = _fox_cum(small_t, _pad_lanes(b_forget[l][None, :], 8).T)
        o_fox = _fox_attn(fox.reshape(b, s, -1), cum_row)

        ik3 = misc3[:, :, IK_OFF:IK_OFF + IDX_DIM].astype(BF16)
        o_dsa = _dsa_attn(dsa_qk.reshape(b, s, -1), dsa_vt, iq.reshape(b, s, -1), ik3, small_t, t5_near, DSA_TQ)

        q_m, k_m, v_m = _mla_prep(misc, _pad_lanes(mla_q_norm[l][None, :], 256), mla_kv_norm[l][None, :],
                                  wqa[l], wqb[l], wk[l], wv[l], ea, eb, cos_t, sin_t, s)
        o_mla = _mla_attn(q_m.reshape(b, s, -1), k_m.reshape(b, s, -1), v_m.reshape(b, s, -1))

        o_band = _band_attn(band.reshape(b, s, -1), _band_bias(chunk_rel_bias[l], BAND_TQ), BAND_TQ)

        mix = [o.reshape(n, GROUP_WIDTH) for o in (o_fox, o_dsa, o_mla, o_band)]
        x2, gates_g = _outproj_ln_router(mix, w_out_b[l], x2, ln1_g[l][None, :], ln1_b[l][None, :],
                                         w_router2[l], router_bias_p[l], alpha)
        x2 = _moe_ln(x2, gates_g, w_gu_b, w_down_b, w_sh_gu_b[l], w_sh_down_b[l],
                     ln2_g[l][None, :], ln2_b[l][None, :], l, alpha)
    return x2.reshape(b, s, d)
```

```python
import functools
import math

import numpy as np
import jax
import jax.numpy as jnp
from jax import lax
from jax.experimental import pallas as pl
from jax.experimental.pallas import tpu as pltpu

D_MODEL = 1024
CHUNK = 64
HEAD_DIM = 64
GROUP_HEADS = 4
GROUP_WIDTH = GROUP_HEADS * HEAD_DIM
IDX_HEADS = 8
IDX_DIM = 32
TOPK_MAX = 256
T5_BUCKETS = 32
T5_MAX_DIST = 128
Q_LORA = 192
KV_LORA = 128
MLA_NOPE = 64
MLA_ROPE = 32
ROPE_THETA = 10000.0
LEFT_CHUNKS = 8
MAX_REL = 256
N_EXPERTS = 64
TOP_K = 8
D_EXPERT = 256
D_SHARED = 256
ROUTED_SCALE = 2.5
MOE_GROUP = 4
BAND_TQ = 256
DSA_TQ = 128
DSA_BUCKET = 256
DSA_NEAR = DSA_BUCKET + T5_MAX_DIST
LN_EPS = 1e-5
RMS_EPS = 1e-6

LANES = 128
NEG = -1e30
BISECT_STEPS = 16
VMEM_LIMIT = 56 * 1024 * 1024

MISC_W = 512
CQ_OFF, CKV_OFF, SMALL_OFF = 0, 256, 384
FF_OFF, IW_OFF, IK_OFF, KR_OFF = 384, 392, 416, 448
SMALL_T_ROWS = 16

BF16 = jnp.bfloat16
F32 = jnp.float32


def _dot(a, b):
    return jnp.dot(a, b, preferred_element_type=F32)


def _dot_nt(a, b):
    return lax.dot_general(a, b, (((1,), (1,)), ((), ())), preferred_element_type=F32)


def _params(*sem):
    return pltpu.CompilerParams(dimension_semantics=sem, vmem_limit_bytes=VMEM_LIMIT)


def _layer_norm(y, g, b):
    mu = jnp.mean(y, axis=-1, keepdims=True)
    yc = y - mu
    var = jnp.mean(yc * yc, axis=-1, keepdims=True)
    return yc * lax.rsqrt(var + LN_EPS) * g + b


def _chunk_end(pos):
    return (jnp.right_shift(pos, CHUNK.bit_length() - 1) + 1) * CHUNK


def _silu(x):
    return x / (1.0 + jnp.exp(-x))


def _softmax_pv(s, v):
    m = jnp.max(s, axis=1, keepdims=True)
    p = jnp.exp(s - m)
    l = jnp.sum(p, axis=1, keepdims=True)
    return _dot(p.astype(BF16), v) / l


def _inproj_kernel(x_ref, w_ref, wt_ref, *refs):
    mla_in = refs[:10]
    fox_ref, dsa_ref, band_ref, iq_ref, ik_ref, q_ref, k_ref, v_ref, vt_ref, small_t_ref = refs[10:]
    xb = x_ref[...].astype(BF16)
    off = 0
    for ref in (fox_ref, dsa_ref, band_ref, iq_ref):
        w = ref.shape[1]
        ref[...] = _dot(xb, w_ref[:, off:off + w]).astype(ref.dtype)
        off += w
    misc = _dot(xb, w_ref[:, off:off + MISC_W])
    ik_ref[...] = misc[:, IK_OFF:IK_OFF + IDX_DIM].astype(ik_ref.dtype)
    _mla_heads(misc, *mla_in, q_ref, k_ref, v_ref)
    t = _dot_nt(wt_ref[...], xb)
    vt_ref[...] = t[:GROUP_WIDTH, :].astype(vt_ref.dtype)
    small_t_ref[...] = t[GROUP_WIDTH:, :]


def _inproj(x2, w_re, w_t, mla_params, cos_t, sin_t, s, tm=512):
    n = x2.shape[0]
    per_seq = s // tm
    hw = GROUP_HEADS * LANES
    widths = (3 * GROUP_WIDTH, 2 * GROUP_WIDTH, 3 * GROUP_WIDTH, IDX_HEADS * IDX_DIM, IDX_DIM, hw, hw, GROUP_WIDTH)
    t_rows = (GROUP_WIDTH, SMALL_T_ROWS)
    t_dtypes = (BF16, F32)
    full = lambda a: pl.BlockSpec(a.shape, lambda i: (0,) * a.ndim)
    return pl.pallas_call(
        _inproj_kernel,
        grid=(n // tm,),
        in_specs=[pl.BlockSpec((tm, D_MODEL), lambda i: (i, 0)), full(w_re), full(w_t)]
                 + [full(a) for a in mla_params]
                 + [pl.BlockSpec((tm, hw), lambda i: (i % per_seq, 0))] * 2,
        out_specs=[pl.BlockSpec((tm, w), lambda i: (i, 0)) for w in widths]
                  + [pl.BlockSpec((None, r, tm), lambda i: (i // per_seq, 0, i % per_seq)) for r in t_rows],
        out_shape=[jax.ShapeDtypeStruct((n, w), BF16) for w in widths]
                  + [jax.ShapeDtypeStruct((n // s, r, s), d) for r, d in zip(t_rows, t_dtypes)],
        compiler_params=_params("parallel"),
        name="inproj",
    )(x2, w_re, w_t, *mla_params, cos_t, sin_t)


def _cum_kernel(f_ref, b_ref, o_ref):
    x = f_ref[...] + b_ref[...]
    acc = jnp.minimum(x, 0.0) - jnp.log(1.0 + jnp.exp(-jnp.abs(x)))
    s = x.shape[1]
    lane = lax.broadcasted_iota(jnp.int32, x.shape, 1)
    d = 1
    while d < s:
        acc = acc + jnp.where(lane >= d, pltpu.roll(acc, d, 1), 0.0)
        d *= 2
    o_ref[...] = acc


def _fox_cum(f_t, b_col):
    b, _, s = f_t.shape
    r = b_col.shape[0]
    return pl.pallas_call(
        _cum_kernel,
        grid=(b,),
        in_specs=[pl.BlockSpec((None, r, s), lambda i: (i, 0, 0)),
                  pl.BlockSpec((r, 1), lambda i: (0, 0))],
        out_specs=pl.BlockSpec((None, r, s), lambda i: (i, 0, 0)),
        out_shape=jax.ShapeDtypeStruct((b, r, s), F32),
        compiler_params=_params("parallel"),
        name="fox_cum",
    )(f_t, b_col)


def _mask_last_block(lg, ok):
    tq = ok.shape[1]
    w = lg.shape[1]
    last = jnp.where(ok, lg[:, w - tq:], NEG)
    return last if w == tq else jnp.concatenate([lg[:, :w - tq], last], axis=1)


def _per_tile(body, n_tiles, tq):
    i = pl.program_id(1)
    for j in range(n_tiles):
        pl.when(i == j)(functools.partial(body, (j + 1) * tq))


def _fox_kernel(q_ref, k_ref, v_ref, ck_ref, o_ref, *, tq):
    def tile(w):
        q = q_ref[...] * HEAD_DIM ** -0.5
        k, v, ck = k_ref[:w, :], v_ref[:w, :], ck_ref[:, :w]
        ok = (lax.broadcasted_iota(jnp.int32, (tq, tq), 1) <= lax.broadcasted_iota(jnp.int32, (tq, tq), 0))
        outs = []
        for h in range(GROUP_HEADS):
            sl = slice(h * HEAD_DIM, (h + 1) * HEAD_DIM)
            lg = _dot_nt(q[:, sl], k[:, sl]) - ck[h:h + 1, :]
            outs.append(_softmax_pv(_mask_last_block(lg, ok), v[:, sl]))
        o_ref[...] = jnp.concatenate(outs, axis=1).astype(o_ref.dtype)

    _per_tile(tile, k_ref.shape[0] // tq, tq)


def _fox_attn(qkv3, cum_row, tq=256):
    b, s, _ = qkv3.shape
    gw = GROUP_WIDTH
    return pl.pallas_call(
        functools.partial(_fox_kernel, tq=tq),
        grid=(b, s // tq),
        in_specs=[pl.BlockSpec((None, tq, gw), lambda bi, i: (bi, i, 0)),
                  pl.BlockSpec((None, s, gw), lambda bi, i: (bi, 0, 1)),
                  pl.BlockSpec((None, s, gw), lambda bi, i: (bi, 0, 2)),
                  pl.BlockSpec((None, 8, s), lambda bi, i: (bi, 0, 0))],
        out_specs=pl.BlockSpec((None, tq, gw), lambda bi, i: (bi, i, 0)),
        out_shape=jax.ShapeDtypeStruct((b, s, gw), BF16),
        compiler_params=_params("parallel", "parallel"),
        name="fox_attn",
    )(qkv3, qkv3, qkv3, cum_row)


def _over_keys(reduce, x, chains=4):
    w, tq = x.shape
    sub = 8
    part = reduce(x.reshape(chains, w // (sub * chains), sub, tq), axis=1)
    return reduce(reduce(part, axis=0), axis=0, keepdims=True)


def _kth_largest(sc_ref, w, lo, hi, searching, kf):
    def count_ge(t):
        return _over_keys(jnp.sum, jnp.where(sc_ref[:w, :] >= t, 1.0, 0.0))

    def halve(_, bracket):
        lo, hi = bracket
        mid = lo + (hi - lo) * 0.5
        more = count_ge(mid) >= kf
        return jnp.where(more, mid, lo), jnp.where(more, hi, mid)

    _, hi = lax.fori_loop(0, BISECT_STEPS, halve, (lo, hi))

    def unresolved(state):
        hi, thr, n_ge, open_, it = state
        return jnp.logical_and(jnp.max(open_) > 0.0, it <= w)

    def peel(state):
        hi, thr, n_ge, open_, it = state
        sc = sc_ref[:w, :]
        cand = _over_keys(jnp.max, jnp.where(sc < hi, sc, -jnp.inf))
        c = count_ge(cand)
        found = (open_ > 0.0) & (c >= kf)
        thr = jnp.where(found, cand, thr)
        n_ge = jnp.where(found, c, n_ge)
        open_ = jnp.where(found, 0.0, open_)
        return jnp.where(open_ > 0.0, cand, hi), thr, n_ge, open_, it + 1

    state = (hi, jnp.full_like(lo, jnp.finfo(F32).min), jnp.full_like(lo, kf),
             jnp.where(searching, 1.0, 0.0), jnp.int32(0))
    _, thr, n_ge, _, _ = lax.while_loop(unresolved, peel, state)
    return thr, n_ge


def _dsa_tile(w, q_ref, k_ref, vt_ref, iq_ref, ik_ref, iwt_ref, t5_ref, o_ref, sc_ref, *, tq, n_sel):
    m = pl.program_id(1)
    limit = _chunk_end(m * tq + lax.broadcasted_iota(jnp.int32, (1, tq), 1))
    valid = lax.broadcasted_iota(jnp.int32, (w, tq), 0) < limit

    iq = iq_ref[...]
    ik = ik_ref[:w, :]
    iw = iwt_ref[...] * (IDX_HEADS ** -0.5 * IDX_DIM ** -0.5)
    score = jnp.zeros((w, tq), F32)
    for h in range(IDX_HEADS):
        d = _dot_nt(ik, iq[:, h * IDX_DIM:(h + 1) * IDX_DIM])
        score = score + jnp.maximum(d, 0.0) * iw[h:h + 1, :]
    score = jnp.where(valid, score, -jnp.inf)
    sc_ref[:w, :] = score

    kf = float(n_sel)
    thr = jnp.full((1, tq), jnp.finfo(F32).min, F32)
    if w > n_sel:
        searching = limit > n_sel
        top = _over_keys(jnp.max, score)
        lo0 = jnp.where(searching, _over_keys(jnp.min, jnp.where(valid, score, jnp.inf)), 0.0)
        hi0 = jnp.where(searching, top + jnp.abs(top) * 2.0 ** -20 + 1e-37, 0.0)
        thr, n_ge = _kth_largest(sc_ref, w, lo0, hi0, searching, kf)

        tie_split = jnp.max(jnp.where(searching & (n_ge > kf), 1.0, 0.0)) > 0.0

        @pl.when(tie_split)
        def _drop_surplus_ties():
            score = sc_ref[:w, :]
            room = kf - _over_keys(jnp.sum, jnp.where(score > thr, 1.0, 0.0))
            blk = DSA_BUCKET
            tri = (lax.broadcasted_iota(jnp.int32, (blk, blk), 1)
                   <= lax.broadcasted_iota(jnp.int32, (blk, blk), 0)).astype(BF16)
            before = jnp.zeros((1, tq), F32)
            for c in range(w // blk):
                sb = score[c * blk:(c + 1) * blk, :]
                eb = jnp.where(sb == thr, 1.0, 0.0)
                inc = _dot(tri, eb.astype(BF16)) + before
                surplus = (eb > 0.0) & (inc - eb >= room)
                sc_ref[c * blk:(c + 1) * blk, :] = jnp.where(surplus, -jnp.inf, sb)
                before = before + jnp.sum(eb, axis=0, keepdims=True)

    mask = sc_ref[:w, :] >= thr

    near = DSA_NEAR
    shift = (w - (m + 1) * tq) // tq
    qs = q_ref[...] * HEAD_DIM ** -0.5
    k = k_ref[:w, :]
    outs = []
    for h in range(GROUP_HEADS):
        sl = slice(h * HEAD_DIM, (h + 1) * HEAD_DIM)
        lg = _dot_nt(k[:, sl], qs[:, sl])
        tab = t5_ref[h, shift]
        if w < near:
            lg = lg + tab[near - w:, :]
        else:
            biased = lg[w - near:, :] + tab
            lg = biased if w == near else jnp.concatenate([lg[:w - near, :], biased], axis=0)
        lg = jnp.where(mask, lg, NEG)
        p = jnp.exp(lg - _over_keys(jnp.max, lg))
        l = _over_keys(jnp.sum, p)
        outs.append(_dot(vt_ref[sl, :w], p.astype(BF16)) / l)
    o_ref[...] = jnp.concatenate(outs, axis=0).T.astype(o_ref.dtype)


def _dsa_kernel(*refs, tq, n_sel):
    m = pl.program_id(1)
    s = refs[-1].shape[0]
    per = DSA_BUCKET // tq
    for j in range(s // DSA_BUCKET):
        pl.when(m // per == j)(functools.partial(_dsa_tile, (j + 1) * DSA_BUCKET, *refs, tq=tq, n_sel=n_sel))


def _dsa_attn(qk3, v_t, iq3, ik3, small_t, t5_near, tq):
    b, s, _ = qk3.shape
    gw = GROUP_WIDTH
    n_sel = min(TOPK_MAX, s // 4)
    return pl.pallas_call(
        functools.partial(_dsa_kernel, tq=tq, n_sel=n_sel),
        grid=(b, s // tq),
        in_specs=[pl.BlockSpec((None, tq, gw), lambda bi, i: (bi, i, 0)),
                  pl.BlockSpec((None, s, gw), lambda bi, i: (bi, 0, 1)),
                  pl.BlockSpec((None, gw, s), lambda bi, i: (bi, 0, 0)),
                  pl.BlockSpec((None, tq, IDX_HEADS * IDX_DIM), lambda bi, i: (bi, i, 0)),
                  pl.BlockSpec((None, s, IDX_DIM), lambda bi, i: (bi, 0, 0)),
                  pl.BlockSpec((None, IDX_HEADS, tq), lambda bi, i: (bi, 1, i)),
                  pl.BlockSpec(t5_near.shape, lambda bi, i: (0, 0, 0, 0))],
        out_specs=pl.BlockSpec((None, tq, gw), lambda bi, i: (bi, i, 0)),
        out_shape=jax.ShapeDtypeStruct((b, s, gw), BF16),
        scratch_shapes=[pltpu.VMEM((s, tq), F32)],
        compiler_params=_params("parallel", "parallel"),
        name="dsa_attn",
    )(qk3, qk3, v_t, iq3, ik3, small_t, t5_near)


def _toeplitz(w, rows, cols):
    length = rows + cols - 1
    wp = jnp.pad(w, ((0, 0), (0, 1)))
    flat = jnp.tile(wp, (1, rows))[:, :rows * length]
    return flat.reshape(w.shape[0], rows, length)[:, :, rows - 1:rows - 1 + cols]


def _t5_bucket(rel):
    half = T5_BUCKETS // 2
    max_exact = half // 2
    n = jnp.abs(rel)
    large = max_exact + (jnp.log(jnp.maximum(n, 1).astype(F32) / max_exact)
                         / math.log(T5_MAX_DIST / max_exact) * (half - max_exact)).astype(jnp.int32)
    large = jnp.minimum(large, half - 1)
    return jnp.where(rel > 0, half, 0) + jnp.where(n < max_exact, n, large)


def _t5_near(t5_rel_bias, tq):
    assert tq >= T5_MAX_DIST and DSA_BUCKET % tq == 0
    near = DSA_NEAR
    far = t5_rel_bias[T5_BUCKETS // 2 - 1]
    tables = []
    for shift in range(DSA_BUCKET // tq):
        rel = (1 + shift) * tq - 1 - np.arange(near + tq - 1, dtype=np.int32)
        vals = jnp.stack([t5_rel_bias[:, h][_t5_bucket(jnp.asarray(rel))] for h in range(GROUP_HEADS)])
        tables.append(_toeplitz((vals - far[:, None]).astype(F32), near, tq))
    return jnp.stack(tables, axis=1)


def _mla_heads(misc, qn_ref, kvn_ref, wqa_ref, wqb_ref, wk_ref, wv_ref, ea_ref, eb_ref,
               cos_ref, sin_ref, q_ref, k_ref, v_ref):
    cq = misc[:, CQ_OFF:CQ_OFF + 256]
    ckv = misc[:, CKV_OFF:CKV_OFF + KV_LORA]
    small = misc[:, SMALL_OFF:SMALL_OFF + 128]
    cqn = cq * lax.rsqrt(jnp.sum(cq * cq, axis=1, keepdims=True) / Q_LORA + RMS_EPS) * qn_ref[...]
    ckvn = ckv * lax.rsqrt(jnp.mean(ckv * ckv, axis=1, keepdims=True) + RMS_EPS) * kvn_ref[...]
    cqb, ckvb = cqn.astype(BF16), ckvn.astype(BF16)
    cos, sin = cos_ref[...], sin_ref[...]
    q = (_dot(cqb, wqa_ref[...]) * cos + _dot(cqb, wqb_ref[...]) * sin) * (MLA_NOPE + MLA_ROPE) ** -0.5
    s_hi = small.astype(BF16)
    s_lo = (small - s_hi.astype(F32)).astype(BF16)
    kr_a = _dot(s_hi, ea_ref[...]) + _dot(s_lo, ea_ref[...])
    kr_b = _dot(s_hi, eb_ref[...]) + _dot(s_lo, eb_ref[...])
    k = (_dot(ckvb, wk_ref[...]) + kr_a) * cos + kr_b * sin
    q_ref[...] = q.astype(BF16)
    k_ref[...] = k.astype(BF16)
    v_ref[...] = _dot(ckvb, wv_ref[...]).astype(BF16)


def _mla_kernel(q_ref, k_ref, v_ref, o_ref, *, tq):
    def tile(w):
        q, k, v = q_ref[...], k_ref[:w, :], v_ref[:w, :]
        row = lax.broadcasted_iota(jnp.int32, (tq, tq), 0)
        col = lax.broadcasted_iota(jnp.int32, (tq, tq), 1)
        ok = col < _chunk_end(row)
        outs = []
        for h in range(GROUP_HEADS):
            hl = slice(h * LANES, (h + 1) * LANES)
            lg = _mask_last_block(_dot_nt(q[:, hl], k[:, hl]), ok)
            outs.append(_softmax_pv(lg, v[:, h * HEAD_DIM:(h + 1) * HEAD_DIM]))
        o_ref[...] = jnp.concatenate(outs, axis=1).astype(o_ref.dtype)

    _per_tile(tile, k_ref.shape[0] // tq, tq)


def _mla_attn(q3, k3, v3, tq=256):
    b, s, hw = q3.shape
    gw = GROUP_WIDTH
    return pl.pallas_call(
        functools.partial(_mla_kernel, tq=tq),
        grid=(b, s // tq),
        in_specs=[pl.BlockSpec((None, tq, hw), lambda bi, i: (bi, i, 0)),
                  pl.BlockSpec((None, s, hw), lambda bi, i: (bi, 0, 0)),
                  pl.BlockSpec((None, s, gw), lambda bi, i: (bi, 0, 0))],
        out_specs=pl.BlockSpec((None, tq, gw), lambda bi, i: (bi, i, 0)),
        out_shape=jax.ShapeDtypeStruct((b, s, gw), BF16),
        compiler_params=_params("parallel", "parallel"),
        name="mla_attn",
    )(q3, k3, v3)


def _mla_weights(w_uq, w_ukv):
    depth = w_uq.shape[0]
    half = MLA_ROPE // 2
    q = w_uq.reshape(depth, Q_LORA, GROUP_HEADS, MLA_NOPE + MLA_ROPE)
    nope, r1, r2 = q[..., :MLA_NOPE], q[..., MLA_NOPE:MLA_NOPE + half], q[..., MLA_NOPE + half:]
    tail = jnp.zeros(q.shape[:-1] + (LANES - MLA_NOPE - 2 * half,), q.dtype)

    def q_rows(parts):
        a = jnp.concatenate(parts, axis=-1).reshape(depth, Q_LORA, GROUP_HEADS * LANES)
        return jnp.pad(a, ((0, 0), (0, 256 - Q_LORA), (0, 0)))

    wqa = q_rows([nope, r1, r2, tail])
    wqb = q_rows([jnp.zeros_like(nope), r2, r1, tail])
    kv = w_ukv.reshape(depth, KV_LORA, GROUP_HEADS, 2 * HEAD_DIM)
    k_nope, v = kv[..., :MLA_NOPE], kv[..., MLA_NOPE:]
    wk = jnp.concatenate([k_nope, jnp.zeros(k_nope.shape[:-1] + (LANES - MLA_NOPE,), kv.dtype)], axis=-1)
    wk = wk.reshape(depth, KV_LORA, GROUP_HEADS * LANES)
    wv = v.reshape(depth, KV_LORA, GROUP_WIDTH)
    return wqa.astype(BF16), wqb.astype(BF16), wk.astype(BF16), wv.astype(BF16)


def _mla_tables(s):
    half = MLA_ROPE // 2
    hw = GROUP_HEADS * LANES
    inv_freq = ROPE_THETA ** (-jnp.arange(half, dtype=F32) / half)
    ang = jnp.arange(s).astype(F32)[:, None] * inv_freq[None, :]
    c, sn = jnp.cos(ang), jnp.sin(ang)
    head_cos = jnp.concatenate([jnp.ones((s, MLA_NOPE), F32), c, c, jnp.zeros((s, LANES - MLA_NOPE - 2 * half), F32)], axis=1)
    head_sin = jnp.concatenate([jnp.zeros((s, MLA_NOPE), F32), -sn, sn, jnp.zeros((s, LANES - MLA_NOPE - 2 * half), F32)], axis=1)
    cos_t = jnp.concatenate([head_cos] * GROUP_HEADS, axis=1)
    sin_t = jnp.concatenate([head_sin] * GROUP_HEADS, axis=1)
    ea = np.zeros((128, hw), np.float32)
    eb = np.zeros((128, hw), np.float32)
    kr = KR_OFF - SMALL_OFF
    for h in range(GROUP_HEADS):
        o = h * LANES + MLA_NOPE
        for j in range(half):
            ea[kr + j, o + j] = 1.0
            ea[kr + half + j, o + half + j] = 1.0
            eb[kr + half + j, o + j] = 1.0
            eb[kr + j, o + half + j] = 1.0
    return cos_t, sin_t, jnp.asarray(ea, BF16), jnp.asarray(eb, BF16)


def _band_kernel(q_ref, k_ref, v_ref, bias_ref, o_ref, *, tq):
    left = LEFT_CHUNKS * CHUNK

    def tile(end):
        first = max(0, end - tq - left)
        w = end - first
        q = q_ref[...] * HEAD_DIM ** -0.5
        k, v = k_ref[first:end, :], v_ref[first:end, :]
        outs = []
        for h in range(GROUP_HEADS):
            sl = slice(h * HEAD_DIM, (h + 1) * HEAD_DIM)
            lg = _dot_nt(q[:, sl], k[:, sl]) + bias_ref[h, :, left + tq - w:]
            outs.append(_softmax_pv(lg, v[:, sl]))
        o_ref[...] = jnp.concatenate(outs, axis=1).astype(o_ref.dtype)

    _per_tile(tile, k_ref.shape[0] // tq, tq)


def _band_attn(qkv3, bias, tq=LANES):
    b, s, _ = qkv3.shape
    gw = GROUP_WIDTH
    return pl.pallas_call(
        functools.partial(_band_kernel, tq=tq),
        grid=(b, s // tq),
        in_specs=[pl.BlockSpec((None, tq, gw), lambda bi, i: (bi, i, 0)),
                  pl.BlockSpec((None, s, gw), lambda bi, i: (bi, 0, 1)),
                  pl.BlockSpec((None, s, gw), lambda bi, i: (bi, 0, 2)),
                  pl.BlockSpec(bias.shape, lambda bi, i: (0, 0, 0))],
        out_specs=pl.BlockSpec((None, tq, gw), lambda bi, i: (bi, i, 0)),
        out_shape=jax.ShapeDtypeStruct((b, s, gw), BF16),
        compiler_params=_params("parallel", "parallel"),
        name="band_attn",
    )(qkv3, qkv3, qkv3, bias)


def _band_bias(rel_table, tq=LANES):
    pad = LEFT_CHUNKS * CHUNK
    win = pad + tq
    i = np.arange(tq)[:, None]
    r = np.arange(win)[None, :]
    kc = r // CHUNK - LEFT_CHUNKS
    qc = i // CHUNK
    in_band = (kc <= qc) & (kc >= qc - LEFT_CHUNKS)
    diag = pad + tq - 1 - np.arange(tq + win - 1)
    bias = _toeplitz(rel_table[:, np.clip(diag, -MAX_REL, MAX_REL) + MAX_REL].astype(F32), tq, win)
    return jnp.where(jnp.asarray(in_band)[None], bias, NEG)


def _outproj_kernel(a_ref, b_ref, c_ref, d_ref, w_ref, x_ref, g_ref, bt_ref, wr_ref, rb_ref,
                    y_ref, gate_ref, *, alpha):
    acc = alpha * x_ref[...]
    for j, ref in enumerate((a_ref, b_ref, c_ref, d_ref)):
        acc = acc + _dot(ref[...], w_ref[j * GROUP_WIDTH:(j + 1) * GROUP_WIDTH, :])
    y = _layer_norm(acc, g_ref[...], bt_ref[...])
    y_ref[...] = y

    y_hi = y.astype(BF16)
    y_lo = (y - y_hi.astype(F32)).astype(BF16)
    logits = _dot_nt(wr_ref[0], y_hi) + (_dot_nt(wr_ref[0], y_lo) + _dot_nt(wr_ref[1], y_hi))
    score = 1.0 / (1.0 + jnp.exp(-logits))
    pick = score + rb_ref[...]
    row = lax.broadcasted_iota(jnp.int32, score.shape, 0).astype(F32)
    raw = jnp.zeros_like(score)
    for _ in range(TOP_K):
        best = jnp.max(pick, axis=0, keepdims=True)
        first = jnp.min(jnp.where(pick == best, row, float(LANES)), axis=0, keepdims=True)
        hit = row == first
        raw = jnp.where(hit, score, raw)
        pick = jnp.where(hit, -jnp.inf, pick)
    gates = (raw / jnp.sum(raw, axis=0, keepdims=True) * ROUTED_SCALE).T
    for grp in range(gate_ref.shape[0]):
        gate_ref[grp] = gates[:, grp * MOE_GROUP:(grp + 1) * MOE_GROUP]


def _outproj_ln_router(mix, w_out, x2, g, bt, w_router2, router_bias, alpha, tm=512):
    n = x2.shape[0]
    n_groups = N_EXPERTS // MOE_GROUP
    full = lambda a: pl.BlockSpec(a.shape, lambda i: (0,) * a.ndim)
    return pl.pallas_call(
        functools.partial(_outproj_kernel, alpha=alpha),
        grid=(n // tm,),
        in_specs=[pl.BlockSpec((tm, GROUP_WIDTH), lambda i: (i, 0))] * 4
                 + [full(w_out), pl.BlockSpec((tm, D_MODEL), lambda i: (i, 0)),
                    full(g), full(bt), full(w_router2), full(router_bias)],
        out_specs=[pl.BlockSpec((tm, D_MODEL), lambda i: (i, 0)),
                   pl.BlockSpec((n_groups, tm, MOE_GROUP), lambda i: (0, i, 0))],
        out_shape=[jax.ShapeDtypeStruct((n, D_MODEL), F32),
                   jax.ShapeDtypeStruct((n_groups, n, MOE_GROUP), F32)],
        compiler_params=_params("parallel"),
        name="outproj_ln_router",
    )(*mix, w_out, x2, g, bt, w_router2, router_bias)


def _moe_kernel(x_ref, gate_ref, wgu_ref, wd_ref, wsgu_ref, wsd_ref, g_ref, bt_ref, o_ref,
                acc_ref, xb_ref, *, alpha, group):
    e = pl.program_id(1)

    @pl.when(e == 0)
    def _():
        xb = x_ref[...].astype(BF16)
        xb_ref[...] = xb
        hs = _dot(xb, wsgu_ref[...])
        a = _silu(hs[:, :D_SHARED]) * hs[:, D_SHARED:]
        acc_ref[...] = _dot(a.astype(BF16), wsd_ref[...])

    xb = xb_ref[...]
    gates = gate_ref[...]
    acts = []
    for j in range(group):
        h = _dot(xb, wgu_ref[j])
        a = _silu(h[:, :D_EXPERT]) * h[:, D_EXPERT:] * gates[:, j:j + 1]
        acts.append(a.astype(BF16))
    w_down = wd_ref[...].reshape(group * D_EXPERT, D_MODEL)
    acc_ref[...] += _dot(jnp.concatenate(acts, axis=1), w_down)

    @pl.when(e == pl.num_programs(1) - 1)
    def _():
        y = alpha * x_ref[...] + acc_ref[...]
        o_ref[...] = _layer_norm(y, g_ref[...], bt_ref[...])


def _moe_ln(x2, gates_g, w_gu, w_down, w_sh_gu, w_sh_down, g, bt, layer, alpha, tm=1024):
    n = x2.shape[0]
    n_groups, _, group = gates_g.shape
    full = lambda a: pl.BlockSpec(a.shape, lambda i, e: (0,) * a.ndim)
    return pl.pallas_call(
        functools.partial(_moe_kernel, alpha=alpha, group=group),
        grid=(n // tm, n_groups),
        in_specs=[pl.BlockSpec((tm, D_MODEL), lambda i, e: (i, 0)),
                  pl.BlockSpec((None, tm, group), lambda i, e: (e, i, 0)),
                  pl.BlockSpec((None, group, D_MODEL, 2 * D_EXPERT), lambda i, e: (layer, e, 0, 0)),
                  pl.BlockSpec((None, group, D_EXPERT, D_MODEL), lambda i, e: (layer, e, 0, 0)),
                  full(w_sh_gu), full(w_sh_down), full(g), full(bt)],
        out_specs=pl.BlockSpec((tm, D_MODEL), lambda i, e: (i, 0)),
        out_shape=jax.ShapeDtypeStruct((n, D_MODEL), F32),
        scratch_shapes=[pltpu.VMEM((tm, D_MODEL), F32), pltpu.VMEM((tm, D_MODEL), BF16)],
        compiler_params=_params("parallel", "arbitrary"),
        name="moe_ln",
    )(x2, gates_g, w_gu, w_down, w_sh_gu, w_sh_down, g, bt)


def _inproj_weights(w_in):
    sizes = (3 * GROUP_WIDTH, GROUP_HEADS, 3 * GROUP_WIDTH, IDX_HEADS * IDX_DIM, IDX_DIM, IDX_HEADS,
             Q_LORA, KV_LORA, MLA_ROPE, 3 * GROUP_WIDTH)
    splits = tuple(int(v) for v in np.cumsum(sizes)[:-1])
    (fox, fox_f, dsa, idx_q, idx_k, idx_w, c_q, c_kv, k_r, band) = jnp.split(w_in, splits, axis=-1)
    pieces, at = [], 0
    for off, part in ((CQ_OFF, c_q), (CKV_OFF, c_kv), (FF_OFF, fox_f), (IW_OFF, idx_w), (IK_OFF, idx_k),
                      (KR_OFF, k_r), (MISC_W, None)):
        pieces.append(jnp.zeros(w_in.shape[:-1] + (off - at,), w_in.dtype))
        if part is not None:
            pieces.append(part)
            at = off + part.shape[-1]
    misc = jnp.concatenate(pieces, axis=-1)
    dsa_qk, dsa_v = dsa[..., :2 * GROUP_WIDTH], dsa[..., 2 * GROUP_WIDTH:]
    w_re = jnp.concatenate([fox, dsa_qk, band, idx_q, misc], axis=-1).astype(BF16)
    w_t = jnp.concatenate([dsa_v, misc[..., SMALL_OFF:SMALL_OFF + SMALL_T_ROWS]], axis=-1)
    return w_re, jnp.swapaxes(w_t, -1, -2).astype(BF16)


def _pad_lanes(a, width, value=0.0):
    return jnp.pad(a, [(0, 0)] * (a.ndim - 1) + [(0, width - a.shape[-1])], constant_values=value)


def kernel(x, w_in, b_forget, mla_q_norm, mla_kv_norm, w_mla_uq, w_mla_ukv, t5_rel_bias, chunk_rel_bias,
           w_out, ln1_g, ln1_b, w_router, router_bias, w_exp_gu, w_exp_down, w_sh_gu, w_sh_down, ln2_g, ln2_b):
    b, s, d = x.shape
    depth = w_in.shape[0]
    n = b * s
    alpha = (2 * depth) ** 0.25

    w_in_re, w_in_t = _inproj_weights(w_in)
    t5_near = _t5_near(t5_rel_bias, DSA_TQ)
    cos_t, sin_t, ea, eb = _mla_tables(s)
    wqa, wqb, wk, wv = _mla_weights(w_mla_uq, w_mla_ukv)
    w_out_b = w_out.astype(BF16)
    w_gu_b, w_down_b = w_exp_gu.astype(BF16), w_exp_down.astype(BF16)
    w_sh_gu_b, w_sh_down_b = w_sh_gu.astype(BF16), w_sh_down.astype(BF16)
    w_router_t = jnp.swapaxes(_pad_lanes(w_router, LANES), -1, -2)
    w_router_hi = w_router_t.astype(BF16)
    w_router_lo = (w_router_t - w_router_hi.astype(F32)).astype(BF16)
    w_router2 = jnp.stack([w_router_hi, w_router_lo], axis=1)
    router_bias_p = _pad_lanes(router_bias, LANES, -jnp.inf)[..., None]

    x2 = x.reshape(n, d)
    for l in range(depth):
        mla_params = (_pad_lanes(mla_q_norm[l][None, :], 256), mla_kv_norm[l][None, :],
                      wqa[l], wqb[l], wk[l], wv[l], ea, eb)
        fox, dsa_qk, band, iq, ik, q_m, k_m, v_m, dsa_vt, small_t = _inproj(
            x2, w_in_re[l], w_in_t[l], mla_params, cos_t, sin_t, s)

        cum_row = _fox_cum(small_t, _pad_lanes(b_forget[l][None, :], 8).T)
        o_fox = _fox_attn(fox.reshape(b, s, -1), cum_row)

        o_dsa = _dsa_attn(dsa_qk.reshape(b, s, -1), dsa_vt, iq.reshape(b, s, -1), ik.reshape(b, s, -1),
                          small_t, t5_near, DSA_TQ)

        o_mla = _mla_attn(q_m.reshape(b, s, -1), k_m.reshape(b, s, -1), v_m.reshape(b, s, -1))

        o_band = _band_attn(band.reshape(b, s, -1), _band_bias(chunk_rel_bias[l], BAND_TQ), BAND_TQ)

        mix = [o.reshape(n, GROUP_WIDTH) for o in (o_fox, o_dsa, o_mla, o_band)]
        x2, gates_g = _outproj_ln_router(mix, w_out_b[l], x2, ln1_g[l][None, :], ln1_b[l][None, :],
                                         w_router2[l], router_bias_p[l], alpha)
        x2 = _moe_ln(x2, gates_g, w_gu_b, w_down_b, w_sh_gu_b[l], w_sh_down_b[l],
                     ln2_g[l][None, :], ln2_b[l][None, :], l, alpha)
    return x2.reshape(b, s, d)
```

```python
import functools
import math

import numpy as np
import jax
import jax.numpy as jnp
from jax import lax
from jax.experimental import pallas as pl
from jax.experimental.pallas import tpu as pltpu

D_MODEL = 1024
CHUNK = 64
HEAD_DIM = 64
GROUP_HEADS = 4
GROUP_WIDTH = GROUP_HEADS * HEAD_DIM
IDX_HEADS = 8
IDX_DIM = 32
TOPK_MAX = 256
T5_BUCKETS = 32
T5_MAX_DIST = 128
Q_LORA = 192
KV_LORA = 128
MLA_NOPE = 64
MLA_ROPE = 32
ROPE_THETA = 10000.0
LEFT_CHUNKS = 8
MAX_REL = 256
N_EXPERTS = 64
TOP_K = 8
D_EXPERT = 256
D_SHARED = 256
ROUTED_SCALE = 2.5
MOE_GROUP = 4
BAND_TQ = 256
DSA_TQ = 128
DSA_BUCKET = 256
DSA_NEAR = DSA_BUCKET + T5_MAX_DIST
LN_EPS = 1e-5
RMS_EPS = 1e-6

LANES = 128
NEG = -1e30
BISECT_STEPS = 16
VMEM_LIMIT = 56 * 1024 * 1024

MISC_W = 512
CQ_OFF, CKV_OFF, SMALL_OFF = 0, 256, 384
FF_OFF, IW_OFF, IK_OFF, KR_OFF = 384, 392, 416, 448
SMALL_T_ROWS = 16

BF16 = jnp.bfloat16
F32 = jnp.float32


def _dot(a, b):
    return jnp.dot(a, b, preferred_element_type=F32)


def _dot_nt(a, b):
    return lax.dot_general(a, b, (((1,), (1,)), ((), ())), preferred_element_type=F32)


def _params(*sem):
    return pltpu.CompilerParams(dimension_semantics=sem, vmem_limit_bytes=VMEM_LIMIT)


def _layer_norm(y, g, b):
    mu = jnp.mean(y, axis=-1, keepdims=True)
    yc = y - mu
    var = jnp.mean(yc * yc, axis=-1, keepdims=True)
    return yc * lax.rsqrt(var + LN_EPS) * g + b


def _chunk_end(pos):
    return (jnp.right_shift(pos, CHUNK.bit_length() - 1) + 1) * CHUNK


def _silu(x):
    return x / (1.0 + jnp.exp(-x))


def _softmax_pv(s, v):
    m = jnp.max(s, axis=1, keepdims=True)
    p = jnp.exp(s - m)
    l = jnp.sum(p, axis=1, keepdims=True)
    return _dot(p.astype(BF16), v) / l


def _inproj_kernel(x_ref, w_ref, wt_ref, *refs):
    mla_in = refs[:10]
    fox_ref, dsa_ref, band_ref, iq_ref, ik_ref, q_ref, k_ref, v_ref, vt_ref, small_t_ref = refs[10:]
    xb = x_ref[...].astype(BF16)
    off = 0
    for ref in (fox_ref, dsa_ref, band_ref, iq_ref):
        w = ref.shape[1]
        ref[...] = _dot(xb, w_ref[:, off:off + w]).astype(ref.dtype)
        off += w
    misc = _dot(xb, w_ref[:, off:off + MISC_W])
    ik_ref[...] = misc[:, IK_OFF:IK_OFF + IDX_DIM].astype(ik_ref.dtype)
    _mla_heads(misc, *mla_in, q_ref, k_ref, v_ref)
    t = _dot_nt(wt_ref[...], xb)
    vt_ref[...] = t[:GROUP_WIDTH, :].astype(vt_ref.dtype)
    small_t_ref[...] = t[GROUP_WIDTH:, :]


def _inproj(x2, w_re, w_t, mla_params, cos_t, sin_t, s, tm=512):
    n = x2.shape[0]
    per_seq = s // tm
    hw = GROUP_HEADS * LANES
    widths = (3 * GROUP_WIDTH, 2 * GROUP_WIDTH, 3 * GROUP_WIDTH, IDX_HEADS * IDX_DIM, IDX_DIM, hw, hw, GROUP_WIDTH)
    t_rows = (GROUP_WIDTH, SMALL_T_ROWS)
    t_dtypes = (BF16, F32)
    full = lambda a: pl.BlockSpec(a.shape, lambda i: (0,) * a.ndim)
    return pl.pallas_call(
        _inproj_kernel,
        grid=(n // tm,),
        in_specs=[pl.BlockSpec((tm, D_MODEL), lambda i: (i, 0)), full(w_re), full(w_t)]
                 + [full(a) for a in mla_params]
                 + [pl.BlockSpec((tm, hw), lambda i: (i % per_seq, 0))] * 2,
        out_specs=[pl.BlockSpec((tm, w), lambda i: (i, 0)) for w in widths]
                  + [pl.BlockSpec((None, r, tm), lambda i: (i // per_seq, 0, i % per_seq)) for r in t_rows],
        out_shape=[jax.ShapeDtypeStruct((n, w), BF16) for w in widths]
                  + [jax.ShapeDtypeStruct((n // s, r, s), d) for r, d in zip(t_rows, t_dtypes)],
        compiler_params=_params("parallel"),
        name="inproj",
    )(x2, w_re, w_t, *mla_params, cos_t, sin_t)


def _cum_kernel(f_ref, b_ref, o_ref):
    x = f_ref[...] + b_ref[...]
    acc = jnp.minimum(x, 0.0) - jnp.log(1.0 + jnp.exp(-jnp.abs(x)))
    s = x.shape[1]
    lane = lax.broadcasted_iota(jnp.int32, x.shape, 1)
    d = 1
    while d < s:
        acc = acc + jnp.where(lane >= d, pltpu.roll(acc, d, 1), 0.0)
        d *= 2
    o_ref[...] = acc


def _fox_cum(f_t, b_col):
    b, _, s = f_t.shape
    r = b_col.shape[0]
    return pl.pallas_call(
        _cum_kernel,
        grid=(b,),
        in_specs=[pl.BlockSpec((None, r, s), lambda i: (i, 0, 0)),
                  pl.BlockSpec((r, 1), lambda i: (0, 0))],
        out_specs=pl.BlockSpec((None, r, s), lambda i: (i, 0, 0)),
        out_shape=jax.ShapeDtypeStruct((b, r, s), F32),
        compiler_params=_params("parallel"),
        name="fox_cum",
    )(f_t, b_col)


def _mask_last_block(lg, ok):
    tq = ok.shape[1]
    w = lg.shape[1]
    last = jnp.where(ok, lg[:, w - tq:], NEG)
    return last if w == tq else jnp.concatenate([lg[:, :w - tq], last], axis=1)


def _per_tile(body, n_tiles, tq):
    i = pl.program_id(1)
    for j in range(n_tiles):
        pl.when(i == j)(functools.partial(body, (j + 1) * tq))


def _fox_kernel(q_ref, k_ref, v_ref, ck_ref, o_ref, *, tq):
    def tile(w):
        q = q_ref[...] * HEAD_DIM ** -0.5
        k, v, ck = k_ref[:w, :], v_ref[:w, :], ck_ref[:, :w]
        ok = (lax.broadcasted_iota(jnp.int32, (tq, tq), 1) <= lax.broadcasted_iota(jnp.int32, (tq, tq), 0))
        outs = []
        for h in range(GROUP_HEADS):
            sl = slice(h * HEAD_DIM, (h + 1) * HEAD_DIM)
            lg = _dot_nt(q[:, sl], k[:, sl]) - ck[h:h + 1, :]
            outs.append(_softmax_pv(_mask_last_block(lg, ok), v[:, sl]))
        o_ref[...] = jnp.concatenate(outs, axis=1).astype(o_ref.dtype)

    _per_tile(tile, k_ref.shape[0] // tq, tq)


def _fox_attn(qkv3, cum_row, tq=256):
    b, s, _ = qkv3.shape
    gw = GROUP_WIDTH
    return pl.pallas_call(
        functools.partial(_fox_kernel, tq=tq),
        grid=(b, s // tq),
        in_specs=[pl.BlockSpec((None, tq, gw), lambda bi, i: (bi, i, 0)),
                  pl.BlockSpec((None, s, gw), lambda bi, i: (bi, 0, 1)),
                  pl.BlockSpec((None, s, gw), lambda bi, i: (bi, 0, 2)),
                  pl.BlockSpec((None, 8, s), lambda bi, i: (bi, 0, 0))],
        out_specs=pl.BlockSpec((None, tq, gw), lambda bi, i: (bi, i, 0)),
        out_shape=jax.ShapeDtypeStruct((b, s, gw), BF16),
        compiler_params=_params("parallel", "parallel"),
        name="fox_attn",
    )(qkv3, qkv3, qkv3, cum_row)


def _over_keys(reduce, x, chains=4):
    w, tq = x.shape
    sub = 8
    part = reduce(x.reshape(chains, w // (sub * chains), sub, tq), axis=1)
    return reduce(reduce(part, axis=0), axis=0, keepdims=True)


def _kth_largest(sc_ref, w, lo, hi, searching, kf):
    def count_ge(t):
        return _over_keys(jnp.sum, jnp.where(sc_ref[:w, :] >= t, 1.0, 0.0))

    def halve(_, bracket):
        lo, hi = bracket
        mid = lo + (hi - lo) * 0.5
        more = count_ge(mid) >= kf
        return jnp.where(more, mid, lo), jnp.where(more, hi, mid)

    _, hi = lax.fori_loop(0, BISECT_STEPS, halve, (lo, hi))

    def unresolved(state):
        hi, thr, n_ge, open_, it = state
        return jnp.logical_and(jnp.max(open_) > 0.0, it <= w)

    def peel(state):
        hi, thr, n_ge, open_, it = state
        sc = sc_ref[:w, :]
        cand = _over_keys(jnp.max, jnp.where(sc < hi, sc, -jnp.inf))
        c = count_ge(cand)
        found = (open_ > 0.0) & (c >= kf)
        thr = jnp.where(found, cand, thr)
        n_ge = jnp.where(found, c, n_ge)
        open_ = jnp.where(found, 0.0, open_)
        return jnp.where(open_ > 0.0, cand, hi), thr, n_ge, open_, it + 1

    state = (hi, jnp.full_like(lo, jnp.finfo(F32).min), jnp.full_like(lo, kf),
             jnp.where(searching, 1.0, 0.0), jnp.int32(0))
    _, thr, n_ge, _, _ = lax.while_loop(unresolved, peel, state)
    return thr, n_ge


def _dsa_tile(w, q_ref, k_ref, vt_ref, iq_ref, ik_ref, iwt_ref, t5_ref, o_ref, sc_ref, *, tq, n_sel):
    m = pl.program_id(1)
    limit = _chunk_end(m * tq + lax.broadcasted_iota(jnp.int32, (1, tq), 1))
    valid = lax.broadcasted_iota(jnp.int32, (w, tq), 0) < limit

    iq = iq_ref[...]
    ik = ik_ref[:w, :]
    iw = iwt_ref[...] * (IDX_HEADS ** -0.5 * IDX_DIM ** -0.5)
    score = jnp.zeros((w, tq), F32)
    for h in range(IDX_HEADS):
        d = _dot_nt(ik, iq[:, h * IDX_DIM:(h + 1) * IDX_DIM])
        score = score + jnp.maximum(d, 0.0) * iw[h:h + 1, :]
    score = jnp.where(valid, score, -jnp.inf)
    sc_ref[:w, :] = score

    kf = float(n_sel)
    thr = jnp.full((1, tq), jnp.finfo(F32).min, F32)
    if w > n_sel:
        searching = limit > n_sel
        top = _over_keys(jnp.max, score)
        lo0 = jnp.where(searching, _over_keys(jnp.min, jnp.where(valid, score, jnp.inf)), 0.0)
        hi0 = jnp.where(searching, top + jnp.abs(top) * 2.0 ** -20 + 1e-37, 0.0)
        thr, n_ge = _kth_largest(sc_ref, w, lo0, hi0, searching, kf)

        tie_split = jnp.max(jnp.where(searching & (n_ge > kf), 1.0, 0.0)) > 0.0

        @pl.when(tie_split)
        def _drop_surplus_ties():
            score = sc_ref[:w, :]
            room = kf - _over_keys(jnp.sum, jnp.where(score > thr, 1.0, 0.0))
            blk = DSA_BUCKET
            tri = (lax.broadcasted_iota(jnp.int32, (blk, blk), 1)
                   <= lax.broadcasted_iota(jnp.int32, (blk, blk), 0)).astype(BF16)
            before = jnp.zeros((1, tq), F32)
            for c in range(w // blk):
                sb = score[c * blk:(c + 1) * blk, :]
                eb = jnp.where(sb == thr, 1.0, 0.0)
                inc = _dot(tri, eb.astype(BF16)) + before
                surplus = (eb > 0.0) & (inc - eb >= room)
                sc_ref[c * blk:(c + 1) * blk, :] = jnp.where(surplus, -jnp.inf, sb)
                before = before + jnp.sum(eb, axis=0, keepdims=True)

    mask = sc_ref[:w, :] >= thr

    near = DSA_NEAR
    shift = (w - (m + 1) * tq) // tq
    qs = q_ref[...] * HEAD_DIM ** -0.5
    k = k_ref[:w, :]
    outs = []
    for h in range(GROUP_HEADS):
        sl = slice(h * HEAD_DIM, (h + 1) * HEAD_DIM)
        lg = _dot_nt(k[:, sl], qs[:, sl])
        tab = t5_ref[h, shift]
        if w < near:
            lg = lg + tab[near - w:, :]
        else:
            biased = lg[w - near:, :] + tab
            lg = biased if w == near else jnp.concatenate([lg[:w - near, :], biased], axis=0)
        lg = jnp.where(mask, lg, NEG)
        p = jnp.exp(lg - _over_keys(jnp.max, lg))
        l = _over_keys(jnp.sum, p)
        outs.append(_dot(vt_ref[sl, :w], p.astype(BF16)) / l)
    o_ref[...] = jnp.concatenate(outs, axis=0).T.astype(o_ref.dtype)


def _dsa_kernel(*refs, tq, n_sel):
    m = pl.program_id(1)
    s = refs[-1].shape[0]
    per = DSA_BUCKET // tq
    for j in range(s // DSA_BUCKET):
        pl.when(m // per == j)(functools.partial(_dsa_tile, (j + 1) * DSA_BUCKET, *refs, tq=tq, n_sel=n_sel))


def _dsa_attn(qk3, v_t, iq3, ik3, small_t, t5_near, tq):
    b, s, _ = qk3.shape
    gw = GROUP_WIDTH
    n_sel = min(TOPK_MAX, s // 4)
    return pl.pallas_call(
        functools.partial(_dsa_kernel, tq=tq, n_sel=n_sel),
        grid=(b, s // tq),
        in_specs=[pl.BlockSpec((None, tq, gw), lambda bi, i: (bi, i, 0)),
                  pl.BlockSpec((None, s, gw), lambda bi, i: (bi, 0, 1)),
                  pl.BlockSpec((None, gw, s), lambda bi, i: (bi, 0, 0)),
                  pl.BlockSpec((None, tq, IDX_HEADS * IDX_DIM), lambda bi, i: (bi, i, 0)),
                  pl.BlockSpec((None, s, IDX_DIM), lambda bi, i: (bi, 0, 0)),
                  pl.BlockSpec((None, IDX_HEADS, tq), lambda bi, i: (bi, 1, i)),
                  pl.BlockSpec(t5_near.shape, lambda bi, i: (0, 0, 0, 0))],
        out_specs=pl.BlockSpec((None, tq, gw), lambda bi, i: (bi, i, 0)),
        out_shape=jax.ShapeDtypeStruct((b, s, gw), BF16),
        scratch_shapes=[pltpu.VMEM((s, tq), F32)],
        compiler_params=_params("parallel", "parallel"),
        name="dsa_attn",
    )(qk3, qk3, v_t, iq3, ik3, small_t, t5_near)


def _toeplitz(w, rows, cols):
    length = rows + cols - 1
    wp = jnp.pad(w, ((0, 0), (0, 1)))
    flat = jnp.tile(wp, (1, rows))[:, :rows * length]
    return flat.reshape(w.shape[0], rows, length)[:, :, rows - 1:rows - 1 + cols]


def _t5_bucket(rel):
    half = T5_BUCKETS // 2
    max_exact = half // 2
    n = jnp.abs(rel)
    large = max_exact + (jnp.log(jnp.maximum(n, 1).astype(F32) / max_exact)
                         / math.log(T5_MAX_DIST / max_exact) * (half - max_exact)).astype(jnp.int32)
    large = jnp.minimum(large, half - 1)
    return jnp.where(rel > 0, half, 0) + jnp.where(n < max_exact, n, large)


def _t5_near(t5_rel_bias, tq):
    assert tq >= T5_MAX_DIST and DSA_BUCKET % tq == 0
    near = DSA_NEAR
    far = t5_rel_bias[T5_BUCKETS // 2 - 1]
    tables = []
    for shift in range(DSA_BUCKET // tq):
        rel = (1 + shift) * tq - 1 - np.arange(near + tq - 1, dtype=np.int32)
        vals = jnp.stack([t5_rel_bias[:, h][_t5_bucket(jnp.asarray(rel))] for h in range(GROUP_HEADS)])
        tables.append(_toeplitz((vals - far[:, None]).astype(F32), near, tq))
    return jnp.stack(tables, axis=1)


def _mla_heads(misc, qn_ref, kvn_ref, wqa_ref, wqb_ref, wk_ref, wv_ref, ea_ref, eb_ref,
               cos_ref, sin_ref, q_ref, k_ref, v_ref):
    cq = misc[:, CQ_OFF:CQ_OFF + 256]
    ckv = misc[:, CKV_OFF:CKV_OFF + KV_LORA]
    small = misc[:, SMALL_OFF:SMALL_OFF + 128]
    cqn = cq * lax.rsqrt(jnp.sum(cq * cq, axis=1, keepdims=True) / Q_LORA + RMS_EPS) * qn_ref[...]
    ckvn = ckv * lax.rsqrt(jnp.mean(ckv * ckv, axis=1, keepdims=True) + RMS_EPS) * kvn_ref[...]
    cqb, ckvb = cqn.astype(BF16), ckvn.astype(BF16)
    cos, sin = cos_ref[...], sin_ref[...]
    q = (_dot(cqb, wqa_ref[...]) * cos + _dot(cqb, wqb_ref[...]) * sin) * (MLA_NOPE + MLA_ROPE) ** -0.5
    s_hi = small.astype(BF16)
    s_lo = (small - s_hi.astype(F32)).astype(BF16)
    kr_a = _dot(s_hi, ea_ref[...]) + _dot(s_lo, ea_ref[...])
    kr_b = _dot(s_hi, eb_ref[...]) + _dot(s_lo, eb_ref[...])
    k = (_dot(ckvb, wk_ref[...]) + kr_a) * cos + kr_b * sin
    q_ref[...] = q.astype(BF16)
    k_ref[...] = k.astype(BF16)
    v_ref[...] = _dot(ckvb, wv_ref[...]).astype(BF16)


def _mla_kernel(q_ref, k_ref, v_ref, o_ref, *, tq):
    def tile(w):
        q, k, v = q_ref[...], k_ref[:w, :], v_ref[:w, :]
        row = lax.broadcasted_iota(jnp.int32, (tq, tq), 0)
        col = lax.broadcasted_iota(jnp.int32, (tq, tq), 1)
        ok = col < _chunk_end(row)
        outs = []
        for h in range(GROUP_HEADS):
            hl = slice(h * LANES, (h + 1) * LANES)
            lg = _mask_last_block(_dot_nt(q[:, hl], k[:, hl]), ok)
            outs.append(_softmax_pv(lg, v[:, h * HEAD_DIM:(h + 1) * HEAD_DIM]))
        o_ref[...] = jnp.concatenate(outs, axis=1).astype(o_ref.dtype)

    _per_tile(tile, k_ref.shape[0] // tq, tq)


def _mla_attn(q3, k3, v3, tq=256):
    b, s, hw = q3.shape
    gw = GROUP_WIDTH
    return pl.pallas_call(
        functools.partial(_mla_kernel, tq=tq),
        grid=(b, s // tq),
        in_specs=[pl.BlockSpec((None, tq, hw), lambda bi, i: (bi, i, 0)),
                  pl.BlockSpec((None, s, hw), lambda bi, i: (bi, 0, 0)),
                  pl.BlockSpec((None, s, gw), lambda bi, i: (bi, 0, 0))],
        out_specs=pl.BlockSpec((None, tq, gw), lambda bi, i: (bi, i, 0)),
        out_shape=jax.ShapeDtypeStruct((b, s, gw), BF16),
        compiler_params=_params("parallel", "parallel"),
        name="mla_attn",
    )(q3, k3, v3)


def _mla_weights(w_uq, w_ukv):
    depth = w_uq.shape[0]
    half = MLA_ROPE // 2
    q = w_uq.reshape(depth, Q_LORA, GROUP_HEADS, MLA_NOPE + MLA_ROPE)
    nope, r1, r2 = q[..., :MLA_NOPE], q[..., MLA_NOPE:MLA_NOPE + half], q[..., MLA_NOPE + half:]
    tail = jnp.zeros(q.shape[:-1] + (LANES - MLA_NOPE - 2 * half,), q.dtype)

    def q_rows(parts):
        a = jnp.concatenate(parts, axis=-1).reshape(depth, Q_LORA, GROUP_HEADS * LANES)
        return jnp.pad(a, ((0, 0), (0, 256 - Q_LORA), (0, 0)))

    wqa = q_rows([nope, r1, r2, tail])
    wqb = q_rows([jnp.zeros_like(nope), r2, r1, tail])
    kv = w_ukv.reshape(depth, KV_LORA, GROUP_HEADS, 2 * HEAD_DIM)
    k_nope, v = kv[..., :MLA_NOPE], kv[..., MLA_NOPE:]
    wk = jnp.concatenate([k_nope, jnp.zeros(k_nope.shape[:-1] + (LANES - MLA_NOPE,), kv.dtype)], axis=-1)
    wk = wk.reshape(depth, KV_LORA, GROUP_HEADS * LANES)
    wv = v.reshape(depth, KV_LORA, GROUP_WIDTH)
    return wqa.astype(BF16), wqb.astype(BF16), wk.astype(BF16), wv.astype(BF16)


def _mla_tables(s):
    half = MLA_ROPE // 2
    hw = GROUP_HEADS * LANES
    inv_freq = ROPE_THETA ** (-jnp.arange(half, dtype=F32) / half)
    ang = jnp.arange(s).astype(F32)[:, None] * inv_freq[None, :]
    c, sn = jnp.cos(ang), jnp.sin(ang)
    head_cos = jnp.concatenate([jnp.ones((s, MLA_NOPE), F32), c, c, jnp.zeros((s, LANES - MLA_NOPE - 2 * half), F32)], axis=1)
    head_sin = jnp.concatenate([jnp.zeros((s, MLA_NOPE), F32), -sn, sn, jnp.zeros((s, LANES - MLA_NOPE - 2 * half), F32)], axis=1)
    cos_t = jnp.concatenate([head_cos] * GROUP_HEADS, axis=1)
    sin_t = jnp.concatenate([head_sin] * GROUP_HEADS, axis=1)
    ea = np.zeros((128, hw), np.float32)
    eb = np.zeros((128, hw), np.float32)
    kr = KR_OFF - SMALL_OFF
    for h in range(GROUP_HEADS):
        o = h * LANES + MLA_NOPE
        for j in range(half):
            ea[kr + j, o + j] = 1.0
            ea[kr + half + j, o + half + j] = 1.0
            eb[kr + half + j, o + j] = 1.0
            eb[kr + j, o + half + j] = 1.0
    return cos_t, sin_t, jnp.asarray(ea, BF16), jnp.asarray(eb, BF16)


def _band_kernel(q_ref, k_ref, v_ref, bias_ref, o_ref, *, tq):
    left = LEFT_CHUNKS * CHUNK

    def tile(end):
        first = max(0, end - tq - left)
        w = end - first
        q = q_ref[...] * HEAD_DIM ** -0.5
        k, v = k_ref[first:end, :], v_ref[first:end, :]
        outs = []
        for h in range(GROUP_HEADS):
            sl = slice(h * HEAD_DIM, (h + 1) * HEAD_DIM)
            lg = _dot_nt(q[:, sl], k[:, sl]) + bias_ref[h, :, left + tq - w:]
            outs.append(_softmax_pv(lg, v[:, sl]))
        o_ref[...] = jnp.concatenate(outs, axis=1).astype(o_ref.dtype)

    _per_tile(tile, k_ref.shape[0] // tq, tq)


def _band_attn(qkv3, bias, tq=LANES):
    b, s, _ = qkv3.shape
    gw = GROUP_WIDTH
    return pl.pallas_call(
        functools.partial(_band_kernel, tq=tq),
        grid=(b, s // tq),
        in_specs=[pl.BlockSpec((None, tq, gw), lambda bi, i: (bi, i, 0)),
                  pl.BlockSpec((None, s, gw), lambda bi, i: (bi, 0, 1)),
                  pl.BlockSpec((None, s, gw), lambda bi, i: (bi, 0, 2)),
                  pl.BlockSpec(bias.shape, lambda bi, i: (0, 0, 0))],
        out_specs=pl.BlockSpec((None, tq, gw), lambda bi, i: (bi, i, 0)),
        out_shape=jax.ShapeDtypeStruct((b, s, gw), BF16),
        compiler_params=_params("parallel", "parallel"),
        name="band_attn",
    )(qkv3, qkv3, qkv3, bias)


def _band_bias(rel_table, tq=LANES):
    pad = LEFT_CHUNKS * CHUNK
    win = pad + tq
    i = np.arange(tq)[:, None]
    r = np.arange(win)[None, :]
    kc = r // CHUNK - LEFT_CHUNKS
    qc = i // CHUNK
    in_band = (kc <= qc) & (kc >= qc - LEFT_CHUNKS)
    diag = pad + tq - 1 - np.arange(tq + win - 1)
    bias = _toeplitz(rel_table[:, np.clip(diag, -MAX_REL, MAX_REL) + MAX_REL].astype(F32), tq, win)
    return jnp.where(jnp.asarray(in_band)[None], bias, NEG)


def _outproj_kernel(a_ref, b_ref, c_ref, d_ref, w_ref, x_ref, g_ref, bt_ref, wr_ref, rb_ref,
                    y_ref, gate_ref, *, alpha):
    acc = alpha * x_ref[...]
    for j, ref in enumerate((a_ref, b_ref, c_ref, d_ref)):
        acc = acc + _dot(ref[...], w_ref[j * GROUP_WIDTH:(j + 1) * GROUP_WIDTH, :])
    y = _layer_norm(acc, g_ref[...], bt_ref[...])
    y_ref[...] = y

    y_hi = y.astype(BF16)
    y_lo = (y - y_hi.astype(F32)).astype(BF16)
    logits = _dot_nt(wr_ref[0], y_hi) + (_dot_nt(wr_ref[0], y_lo) + _dot_nt(wr_ref[1], y_hi))
    score = 1.0 / (1.0 + jnp.exp(-logits))
    pick = score + rb_ref[...]
    row = lax.broadcasted_iota(jnp.int32, score.shape, 0).astype(F32)
    raw = jnp.zeros_like(score)
    for _ in range(TOP_K):
        best = jnp.max(pick, axis=0, keepdims=True)
        first = jnp.min(jnp.where(pick == best, row, float(LANES)), axis=0, keepdims=True)
        hit = row == first
        raw = jnp.where(hit, score, raw)
        pick = jnp.where(hit, -jnp.inf, pick)
    gates = (raw / jnp.sum(raw, axis=0, keepdims=True) * ROUTED_SCALE).T
    for grp in range(gate_ref.shape[0]):
        gate_ref[grp] = gates[:, grp * MOE_GROUP:(grp + 1) * MOE_GROUP]


def _outproj_ln_router(mix, w_out, x2, g, bt, w_router2, router_bias, alpha, tm=1024):
    n = x2.shape[0]
    n_groups = N_EXPERTS // MOE_GROUP
    full = lambda a: pl.BlockSpec(a.shape, lambda i: (0,) * a.ndim)
    return pl.pallas_call(
        functools.partial(_outproj_kernel, alpha=alpha),
        grid=(n // tm,),
        in_specs=[pl.BlockSpec((tm, GROUP_WIDTH), lambda i: (i, 0))] * 4
                 + [full(w_out), pl.BlockSpec((tm, D_MODEL), lambda i: (i, 0)),
                    full(g), full(bt), full(w_router2), full(router_bias)],
        out_specs=[pl.BlockSpec((tm, D_MODEL), lambda i: (i, 0)),
                   pl.BlockSpec((n_groups, tm, MOE_GROUP), lambda i: (0, i, 0))],
        out_shape=[jax.ShapeDtypeStruct((n, D_MODEL), F32),
                   jax.ShapeDtypeStruct((n_groups, n, MOE_GROUP), F32)],
        compiler_params=_params("parallel"),
        name="outproj_ln_router",
    )(*mix, w_out, x2, g, bt, w_router2, router_bias)


def _moe_kernel(x_ref, gate_ref, wgu_ref, wd_ref, wsgu_ref, wsd_ref, g_ref, bt_ref, o_ref,
                acc_ref, xb_ref, *, alpha, group):
    e = pl.program_id(1)

    @pl.when(e == 0)
    def _():
        xb = x_ref[...].astype(BF16)
        xb_ref[...] = xb
        hs = _dot(xb, wsgu_ref[...])
        a = _silu(hs[:, :D_SHARED]) * hs[:, D_SHARED:]
        acc_ref[...] = _dot(a.astype(BF16), wsd_ref[...])

    xb = xb_ref[...]
    gates = gate_ref[...]
    acts = []
    for j in range(group):
        h = _dot(xb, wgu_ref[j])
        a = _silu(h[:, :D_EXPERT]) * h[:, D_EXPERT:] * gates[:, j:j + 1]
        acts.append(a.astype(BF16))
    w_down = wd_ref[...].reshape(group * D_EXPERT, D_MODEL)
    acc_ref[...] += _dot(jnp.concatenate(acts, axis=1), w_down)

    @pl.when(e == pl.num_programs(1) - 1)
    def _():
        y = alpha * x_ref[...] + acc_ref[...]
        o_ref[...] = _layer_norm(y, g_ref[...], bt_ref[...])


def _moe_ln(x2, gates_g, w_gu, w_down, w_sh_gu, w_sh_down, g, bt, layer, alpha, tm=1024):
    n = x2.shape[0]
    n_groups, _, group = gates_g.shape
    full = lambda a: pl.BlockSpec(a.shape, lambda i, e: (0,) * a.ndim)
    return pl.pallas_call(
        functools.partial(_moe_kernel, alpha=alpha, group=group),
        grid=(n // tm, n_groups),
        in_specs=[pl.BlockSpec((tm, D_MODEL), lambda i, e: (i, 0)),
                  pl.BlockSpec((None, tm, group), lambda i, e: (e, i, 0)),
                  pl.BlockSpec((None, group, D_MODEL, 2 * D_EXPERT), lambda i, e: (layer, e, 0, 0)),
                  pl.BlockSpec((None, group, D_EXPERT, D_MODEL), lambda i, e: (layer, e, 0, 0)),
                  full(w_sh_gu), full(w_sh_down), full(g), full(bt)],
        out_specs=pl.BlockSpec((tm, D_MODEL), lambda i, e: (i, 0)),
        out_shape=jax.ShapeDtypeStruct((n, D_MODEL), F32),
        scratch_shapes=[pltpu.VMEM((tm, D_MODEL), F32), pltpu.VMEM((tm, D_MODEL), BF16)],
        compiler_params=_params("parallel", "arbitrary"),
        name="moe_ln",
    )(x2, gates_g, w_gu, w_down, w_sh_gu, w_sh_down, g, bt)


def _inproj_weights(w_in):
    sizes = (3 * GROUP_WIDTH, GROUP_HEADS, 3 * GROUP_WIDTH, IDX_HEADS * IDX_DIM, IDX_DIM, IDX_HEADS,
             Q_LORA, KV_LORA, MLA_ROPE, 3 * GROUP_WIDTH)
    splits = tuple(int(v) for v in np.cumsum(sizes)[:-1])
    (fox, fox_f, dsa, idx_q, idx_k, idx_w, c_q, c_kv, k_r, band) = jnp.split(w_in, splits, axis=-1)
    pieces, at = [], 0
    for off, part in ((CQ_OFF, c_q), (CKV_OFF, c_kv), (FF_OFF, fox_f), (IW_OFF, idx_w), (IK_OFF, idx_k),
                      (KR_OFF, k_r), (MISC_W, None)):
        pieces.append(jnp.zeros(w_in.shape[:-1] + (off - at,), w_in.dtype))
        if part is not None:
            pieces.append(part)
            at = off + part.shape[-1]
    misc = jnp.concatenate(pieces, axis=-1)
    dsa_qk, dsa_v = dsa[..., :2 * GROUP_WIDTH], dsa[..., 2 * GROUP_WIDTH:]
    w_re = jnp.concatenate([fox, dsa_qk, band, idx_q, misc], axis=-1).astype(BF16)
    w_t = jnp.concatenate([dsa_v, misc[..., SMALL_OFF:SMALL_OFF + SMALL_T_ROWS]], axis=-1)
    return w_re, jnp.swapaxes(w_t, -1, -2).astype(BF16)


def _pad_lanes(a, width, value=0.0):
    return jnp.pad(a, [(0, 0)] * (a.ndim - 1) + [(0, width - a.shape[-1])], constant_values=value)


def kernel(x, w_in, b_forget, mla_q_norm, mla_kv_norm, w_mla_uq, w_mla_ukv, t5_rel_bias, chunk_rel_bias,
           w_out, ln1_g, ln1_b, w_router, router_bias, w_exp_gu, w_exp_down, w_sh_gu, w_sh_down, ln2_g, ln2_b):
    b, s, d = x.shape
    depth = w_in.shape[0]
    n = b * s
    alpha = (2 * depth) ** 0.25

    w_in_re, w_in_t = _inproj_weights(w_in)
    t5_near = _t5_near(t5_rel_bias, DSA_TQ)
    cos_t, sin_t, ea, eb = _mla_tables(s)
    wqa, wqb, wk, wv = _mla_weights(w_mla_uq, w_mla_ukv)
    w_out_b = w_out.astype(BF16)
    w_gu_b, w_down_b = w_exp_gu.astype(BF16), w_exp_down.astype(BF16)
    w_sh_gu_b, w_sh_down_b = w_sh_gu.astype(BF16), w_sh_down.astype(BF16)
    w_router_t = jnp.swapaxes(_pad_lanes(w_router, LANES), -1, -2)
    w_router_hi = w_router_t.astype(BF16)
    w_router_lo = (w_router_t - w_router_hi.astype(F32)).astype(BF16)
    w_router2 = jnp.stack([w_router_hi, w_router_lo], axis=1)
    router_bias_p = _pad_lanes(router_bias, LANES, -jnp.inf)[..., None]

    x2 = x.reshape(n, d)
    for l in range(depth):
        mla_params = (_pad_lanes(mla_q_norm[l][None, :], 256), mla_kv_norm[l][None, :],
                      wqa[l], wqb[l], wk[l], wv[l], ea, eb)
        fox, dsa_qk, band, iq, ik, q_m, k_m, v_m, dsa_vt, small_t = _inproj(
            x2, w_in_re[l], w_in_t[l], mla_params, cos_t, sin_t, s)

        cum_row = _fox_cum(small_t, _pad_lanes(b_forget[l][None, :], 8).T)
        o_fox = _fox_attn(fox.reshape(b, s, -1), cum_row)

        o_dsa = _dsa_attn(dsa_qk.reshape(b, s, -1), dsa_vt, iq.reshape(b, s, -1), ik.reshape(b, s, -1),
                          small_t, t5_near, DSA_TQ)

        o_mla = _mla_attn(q_m.reshape(b, s, -1), k_m.reshape(b, s, -1), v_m.reshape(b, s, -1))

        o_band = _band_attn(band.reshape(b, s, -1), _band_bias(chunk_rel_bias[l], BAND_TQ), BAND_TQ)

        mix = [o.reshape(n, GROUP_WIDTH) for o in (o_fox, o_dsa, o_mla, o_band)]
        x2, gates_g = _outproj_ln_router(mix, w_out_b[l], x2, ln1_g[l][None, :], ln1_b[l][None, :],
                                         w_router2[l], router_bias_p[l], alpha)
        x2 = _moe_ln(x2, gates_g, w_gu_b, w_down_b, w_sh_gu_b[l], w_sh_down_b[l],
                     ln2_g[l][None, :], ln2_b[l][None, :], l, alpha)
    return x2.reshape(b, s, d)
```
